```python
import jax
import jax.numpy as jnp
from jax import lax
import numpy as np

D_MODEL = 1024
BATCH = 2
SEQ = 8192
DEPTH = 2

CHUNK = 64
NORM_EPS = 1e-6
ROPE_BASE = 10000.0

RET_HEADS = 4
RET_DK = 128
RET_DV = 128
RET_QK_W = RET_HEADS * RET_DK
RET_W = RET_HEADS * RET_DV
RET_GN_EPS = 1e-5

RWKV_HEAD = 64
RWKV_HEADS = 8
RWKV_W = RWKV_HEADS * RWKV_HEAD
RWKV_DECAY_LORA = 64
RWKV_AAA_LORA = 64
RWKV_GATE_LORA = 160
RWKV_GN_EPS = 64e-5
RWKV_SPLITS = (RWKV_W, 2 * RWKV_W, 3 * RWKV_W, 3 * RWKV_W + RWKV_DECAY_LORA, 3 * RWKV_W + RWKV_DECAY_LORA + RWKV_AAA_LORA)
RWKV_COLS = 3 * RWKV_W + RWKV_DECAY_LORA + RWKV_AAA_LORA + RWKV_GATE_LORA

MLA_HEADS = 8
MLA_Q_LORA = 256
MLA_KV_LORA = 128
MLA_NOPE = 64
MLA_ROPE = 32
MLA_V = 64
MLA_W = MLA_HEADS * MLA_V
Q_BLOCK = 128

N_BRANCH = 3
BRANCH_W = 512

IN_SIZES = (RET_QK_W, RET_QK_W, RET_W, RET_W, RWKV_COLS, MLA_Q_LORA, MLA_KV_LORA + MLA_ROPE, N_BRANCH * D_MODEL)
IN_SPLITS = tuple(np.cumsum(IN_SIZES)[:-1].tolist())
IN_COLS = sum(IN_SIZES)

D_FF = 2816
N_EXPERTS = 8
TOP_K = 2
D_FF_EXPERT = 3584
N_DENSE = (DEPTH + 1) // 2
N_MOE = DEPTH // 2

kernel_name = "hybrid_retention_rwkv7_mla_moe_encoder"


def rms_norm(x, g):
    xf = x.astype(jnp.float32)
    y = xf * lax.rsqrt(jnp.mean(xf * xf, axis=-1, keepdims=True) + NORM_EPS)
    return (y * g.astype(jnp.float32)).astype(x.dtype)


def group_norm(x, eps):
    xc = x - jnp.mean(x, axis=-1, keepdims=True)
    return xc * lax.rsqrt(jnp.mean(xc * xc, axis=-1, keepdims=True) + eps)


def rope_tables(positions, dim):
    inv_freq = 1.0 / (ROPE_BASE ** (jnp.arange(0, dim, 2, dtype=jnp.float32) / dim))
    ang = positions.astype(jnp.float32)[..., None] * inv_freq
    return jnp.cos(ang), jnp.sin(ang)


def apply_rope(x, cos, sin):
    x1, x2 = jnp.split(x, 2, axis=-1)
    c = cos[:, :, None, :]
    s = sin[:, :, None, :]
    return jnp.concatenate([x1 * c - x2 * s, x1 * s + x2 * c], axis=-1).astype(x.dtype)


def token_shift(p):
    return jnp.pad(p, ((0, 0), (1, 0), (0, 0)))[:, :-1]


def retention_mix(q, k, v, g, cos, sin):
    f32 = jnp.float32
    B, S, _ = q.shape
    N, H = S // CHUNK, RET_HEADS
    q = apply_rope(q.reshape(B, S, H, RET_DK), cos, sin).astype(f32)
    k = apply_rope(k.reshape(B, S, H, RET_DK), cos, sin).astype(f32) * (RET_DK ** -0.5)
    v = v.reshape(B, S, H, RET_DV).astype(f32)
    log_gamma = jnp.log(1.0 - 2.0 ** (-5.0 - jnp.arange(H, dtype=f32)))
    idx = jnp.arange(CHUNK, dtype=f32)
    intra = jnp.exp(log_gamma[:, None, None] * jnp.abs(idx[:, None] - idx[None, :]))
    q_decay = jnp.exp(log_gamma[None, :] * (idx[:, None] + 1.0))
    k_decay = jnp.exp(log_gamma[None, :] * (CHUNK - 1.0 - idx[:, None]))
    chunk_decay = jnp.exp(log_gamma * CHUNK)
    qc = q.reshape(B, N, CHUNK, H, RET_DK)
    kc = k.reshape(B, N, CHUNK, H, RET_DK)
    vc = v.reshape(B, N, CHUNK, H, RET_DV)
    scores = jnp.einsum('bnihd,bnjhd->bnhij', qc, kc) * intra
    o = jnp.einsum('bnhij,bnjhe->bnihe', scores, vc)
    kv = jnp.einsum('bnjhd,bnjhe->bnhde', kc * k_decay[:, :, None], vc)

    def step(state, kv_n):
        return state * chunk_decay[None, :, None, None] + kv_n, state

    _, prev = lax.scan(step, jnp.zeros((B, H, RET_DK, RET_DV), f32), jnp.moveaxis(kv, 1, 0))
    prev = jnp.moveaxis(prev, 0, 1)
    o = o + jnp.einsum('bnihd,bnhde->bnihe', qc, prev) * q_decay[:, :, None]
    o = group_norm(o.reshape(B, S, H, RET_DV), RET_GN_EPS).reshape(B, S, RET_W)
    return o * jax.nn.silu(g.astype(f32))


def rwkv7_mix(p, mu, w0, w2, a0, a2, g2, k_k, k_a, r_k, lnx_w):
    f32 = jnp.float32
    B, S, _ = p.shape
    p = p.astype(f32)
    xs = p + (token_shift(p) - p) * mu.astype(f32)
    r, k, v, xw, xa, xg = jnp.split(xs, RWKV_SPLITS, axis=-1)
    w_log = -jax.nn.softplus(-(w0.astype(f32) + jnp.tanh(xw) @ w2.astype(f32))) - 0.5
    decay = jnp.exp(-jnp.exp(w_log))
    a = jax.nn.sigmoid(a0.astype(f32) + xa @ a2.astype(f32))
    g = jax.nn.sigmoid(xg) @ g2.astype(f32)
    shp = (B, S, RWKV_HEADS, RWKV_HEAD)
    kk = (k * k_k.astype(f32)).reshape(shp)
    kk = kk * lax.rsqrt(jnp.sum(kk * kk, axis=-1, keepdims=True) + 1e-12)
    k = k * (1.0 + (a - 1.0) * k_a.astype(f32))
    r, k, v, decay, a = (t.reshape(shp) for t in (r, k, v, decay, a))
    b = kk * a

    def step(state, inp):
        r_t, w_t, k_t, v_t, kk_t, b_t = inp
        sa = -jnp.einsum('bhvk,bhk->bhv', state, kk_t)
        state = state * w_t[:, :, None, :] + sa[..., None] * b_t[:, :, None, :] + v_t[..., None] * k_t[:, :, None, :]
        return state, jnp.einsum('bhvk,bhk->bhv', state, r_t)

    seq = tuple(jnp.moveaxis(t, 1, 0) for t in (r, decay, k, v, kk, b))
    _, y = lax.scan(step, jnp.zeros((B, RWKV_HEADS, RWKV_HEAD, RWKV_HEAD), f32), seq)
    y = jnp.moveaxis(y, 0, 1)
    y = group_norm(y, RWKV_GN_EPS) * lnx_w.astype(f32).reshape(RWKV_HEADS, RWKV_HEAD)
    y = y + jnp.sum(r * k * r_k.astype(f32), axis=-1, keepdims=True) * v
    return y.reshape(B, S, RWKV_W) * g


def mla_mix(qa, kva, q_norm, w_qb, kv_norm, w_kvb, cos, sin):
    B, S, _ = qa.shape
    q = (rms_norm(qa, q_norm) @ w_qb).reshape(B, S, MLA_HEADS, MLA_NOPE + MLA_ROPE)
    q_nope = q[..., :MLA_NOPE]
    q_pe = apply_rope(q[..., MLA_NOPE:], cos, sin)
    c_kv, k_pe = kva[..., :MLA_KV_LORA], kva[..., MLA_KV_LORA:]
    kv = (rms_norm(c_kv, kv_norm) @ w_kvb).reshape(B, S, MLA_HEADS, MLA_NOPE + MLA_V)
    k_nope, v = kv[..., :MLA_NOPE], kv[..., MLA_NOPE:]
    k_pe = apply_rope(k_pe[:, :, None, :], cos, sin)[:, :, 0, :]
    scale = (MLA_NOPE + MLA_ROPE) ** -0.5
    outs = []
    for i in range(S // Q_BLOCK):
        q0, k_end = i * Q_BLOCK, (i + 1) * Q_BLOCK
        s = (jnp.einsum('bqhd,bkhd->bhqk', q_nope[:, q0:k_end], k_nope[:, :k_end])
             + jnp.einsum('bqhr,bkr->bhqk', q_pe[:, q0:k_end], k_pe[:, :k_end])).astype(jnp.float32) * scale
        q_chunk = (q0 + np.arange(Q_BLOCK)) // CHUNK
        k_chunk = np.arange(k_end) // CHUNK
        s = jnp.where(k_chunk[None, :] <= q_chunk[:, None], s, -1e30)
        pr = jax.nn.softmax(s, axis=-1).astype(v.dtype)
        outs.append(jnp.einsum('bhqk,bkhe->bqhe', pr, v[:, :k_end]))
    return jnp.concatenate(outs, axis=1).reshape(B, S, MLA_W)


def swiglu(h, wg, wu, wd):
    return (jax.nn.silu(h @ wg) * (h @ wu)) @ wd


def moe_swiglu(h, router, wg, wu, wd):
    logits = (h @ router).astype(jnp.float32)
    top_v, top_i = lax.top_k(logits, TOP_K)
    top_w = jax.nn.softmax(top_v, axis=-1)
    comb = jnp.sum(jax.nn.one_hot(top_i, N_EXPERTS, dtype=jnp.float32) * top_w[..., None], axis=-2)
    out = jnp.zeros(h.shape, jnp.float32)
    for e in range(N_EXPERTS):
        out = out + comb[..., e:e + 1] * swiglu(h, wg[e], wu[e], wd[e]).astype(jnp.float32)
    return out.astype(h.dtype)


def setup_inputs(seed: int = 0) -> dict:
    key = jax.random.key(seed)
    ks = iter(jax.random.split(key, 40))
    f32 = jnp.float32

    def nrm(shape, scale):
        return jax.random.normal(next(ks), shape, f32) * scale

    def gain(shape):
        return 1.0 + nrm(shape, 0.05)

    L = DEPTH
    x = nrm((BATCH, SEQ, D_MODEL), 1.0)
    start = jax.random.randint(next(ks), (BATCH, 1), 0, 4096, dtype=jnp.int32)
    positions = start + jnp.arange(SEQ, dtype=jnp.int32)[None, :]
    rwkv_w0 = jnp.linspace(-6.5, -1.5, RWKV_W, dtype=f32)[None, :] + nrm((L, RWKV_W), 0.1)
    return {
        'x': x,
        'positions': positions,
        'attn_norm': gain((L, D_MODEL)),
        'w_in': nrm((L, D_MODEL, IN_COLS), D_MODEL ** -0.5),
        'rwkv_mu': jax.random.uniform(next(ks), (L, RWKV_COLS), f32),
        'rwkv_w0': rwkv_w0,
        'rwkv_w2': nrm((L, RWKV_DECAY_LORA, RWKV_W), 0.1),
        'rwkv_a0': nrm((L, RWKV_W), 0.1),
        'rwkv_a2': nrm((L, RWKV_AAA_LORA, RWKV_W), RWKV_AAA_LORA ** -0.5),
        'rwkv_g2': nrm((L, RWKV_GATE_LORA, RWKV_W), RWKV_GATE_LORA ** -0.5),
        'rwkv_k_k': 0.85 + nrm((L, RWKV_W), 0.05),
        'rwkv_k_a': gain((L, RWKV_W)),
        'rwkv_r_k': nrm((L, RWKV_HEADS, RWKV_HEAD), 0.1),
        'rwkv_lnx_w': gain((L, RWKV_W)),
        'mla_q_norm': gain((L, MLA_Q_LORA)),
        'mla_w_qb': nrm((L, MLA_Q_LORA, MLA_HEADS * (MLA_NOPE + MLA_ROPE)), MLA_Q_LORA ** -0.5),
        'mla_kv_norm': gain((L, MLA_KV_LORA)),
        'mla_w_kvb': nrm((L, MLA_KV_LORA, MLA_HEADS * (MLA_NOPE + MLA_V)), MLA_KV_LORA ** -0.5),
        'w_branch': nrm((L, N_BRANCH, BRANCH_W, D_MODEL), BRANCH_W ** -0.5),
        'w_o': nrm((L, D_MODEL, D_MODEL), D_MODEL ** -0.5),
        'ffn_norm': gain((L, D_MODEL)),
        'ffn_w_gate': nrm((N_DENSE, D_MODEL, D_FF), D_MODEL ** -0.5),
        'ffn_w_up': nrm((N_DENSE, D_MODEL, D_FF), D_MODEL ** -0.5),
        'ffn_w_down': nrm((N_DENSE, D_FF, D_MODEL), D_FF ** -0.5),
        'moe_router': nrm((N_MOE, D_MODEL, N_EXPERTS), D_MODEL ** -0.5),
        'moe_w_gate': nrm((N_MOE, N_EXPERTS, D_MODEL, D_FF_EXPERT), D_MODEL ** -0.5),
        'moe_w_up': nrm((N_MOE, N_EXPERTS, D_MODEL, D_FF_EXPERT), D_MODEL ** -0.5),
        'moe_w_down': nrm((N_MOE, N_EXPERTS, D_FF_EXPERT, D_MODEL), D_FF_EXPERT ** -0.5),
        'final_norm': gain((D_MODEL,)),
    }


def reference(x, positions, attn_norm, w_in, rwkv_mu, rwkv_w0, rwkv_w2, rwkv_a0, rwkv_a2, rwkv_g2,
              rwkv_k_k, rwkv_k_a, rwkv_r_k, rwkv_lnx_w, mla_q_norm, mla_w_qb, mla_kv_norm, mla_w_kvb,
              w_branch, w_o, ffn_norm, ffn_w_gate, ffn_w_up, ffn_w_down, moe_router, moe_w_gate,
              moe_w_up, moe_w_down, final_norm):
    B, S, _ = x.shape
    dt = x.dtype
    cos_ret, sin_ret = rope_tables(positions, RET_DK)
    cos_mla, sin_mla = rope_tables(positions, MLA_ROPE)
    h = x
    for l in range(DEPTH):
        hn = rms_norm(h, attn_norm[l])
        p = hn @ w_in[l]
        rq, rk, rv, rg, pw, qa, kva, gl = jnp.split(p, IN_SPLITS, axis=-1)
        y_ret = retention_mix(rq, rk, rv, rg, cos_ret, sin_ret)
        y_rwkv = rwkv7_mix(pw, rwkv_mu[l], rwkv_w0[l], rwkv_w2[l], rwkv_a0[l], rwkv_a2[l], rwkv_g2[l],
                           rwkv_k_k[l], rwkv_k_a[l], rwkv_r_k[l], rwkv_lnx_w[l])
        y_mla = mla_mix(qa, kva, mla_q_norm[l], mla_w_qb[l], mla_kv_norm[l], mla_w_kvb[l], cos_mla, sin_mla)
        ys = jnp.stack([y_ret.astype(dt), y_rwkv.astype(dt), y_mla.astype(dt)], axis=2)
        branch = jnp.einsum('bsnc,ncd->bsnd', ys, w_branch[l])
        gates = jax.nn.sigmoid(gl.reshape(B, S, N_BRANCH, D_MODEL))
        merged = jnp.sum(gates * branch, axis=2)
        h = h + (merged @ w_o[l]).astype(dt)
        hn = rms_norm(h, ffn_norm[l])
        if l % 2 == 0:
            f = swiglu(hn, ffn_w_gate[l // 2], ffn_w_up[l // 2], ffn_w_down[l // 2])
        else:
            f = moe_swiglu(hn, moe_router[l // 2], moe_w_gate[l // 2], moe_w_up[l // 2], moe_w_down[l // 2])
        h = h + f.astype(dt)
    return rms_norm(h, final_norm)
```

```python
import functools

import numpy as np
import jax
import jax.numpy as jnp
from jax import lax
from jax.experimental import pallas as pl
from jax.experimental.pallas import tpu as pltpu

F32 = jnp.float32
BF16 = jnp.bfloat16

D_MODEL = 1024
CHUNK = 64
NORM_EPS = 1e-6
ROPE_BASE = 10000.0

RET_HEADS = 4
RET_DK = 128
RET_GN_EPS = 1e-5

RWKV_HEAD = 64
RWKV_HEADS = 8
RWKV_W = 512
RWKV_GN_EPS = 64e-5
RWKV_COLS = 1824

MLA_HEADS = 8
MLA_Q_LORA = 256
MLA_KV_LORA = 128
MLA_NOPE = 64
MLA_ROPE = 32
MLA_V = 64

N_EXPERTS = 8
D_FF = 2816
D_FF_EXPERT = 3584

P_RET = 0
P_RWKV = 2048
P_GATE = 4096
P_MLA = 7168
P_COLS = 7680

BLK = 256
VMEM_LIMIT_BYTES = 56 * 1024 * 1024


def _cp(*sem):
    return pltpu.CompilerParams(dimension_semantics=sem, vmem_limit_bytes=VMEM_LIMIT_BYTES)


def _sigmoid(z):
    return 1.0 / (1.0 + jnp.exp(-z))


def _split3(x):
    hi = x.astype(BF16)
    r1 = x - hi.astype(F32)
    mid = r1.astype(BF16)
    lo = (r1 - mid.astype(F32)).astype(BF16)
    return hi, mid, lo


def _dot(a, b):
    return jnp.dot(a, b, preferred_element_type=F32)


def _dot_nt(a, b):
    return lax.dot_general(a, b, (((1,), (1,)), ((), ())), preferred_element_type=F32)


def _dot_tn(a, b):
    return lax.dot_general(a, b, (((0,), (0,)), ((), ())), preferred_element_type=F32)


def _dot_exact_lhs(m_bf16, x_f32):
    hi, mid, lo = _split3(x_f32)
    return _dot(m_bf16, hi) + _dot(m_bf16, mid) + _dot(m_bf16, lo)


def _dot_exact_rhs(x_f32, m_bf16):
    hi, mid, lo = _split3(x_f32)
    return _dot(hi, m_bf16) + _dot(mid, m_bf16) + _dot(lo, m_bf16)


TAB_COLS = 768


def _tables_kernel(pos_ref, freq_ref, perm_ref, o_ref):
    ang = pos_ref[...] * freq_ref[...]
    cs = jnp.concatenate([jnp.cos(ang), jnp.sin(ang)], axis=1)
    o_ref[...] = _dot_exact_rhs(cs, perm_ref[...])


def _table_perm():
    p = np.zeros((256, TAB_COLS), np.float32)
    c, s = 0, 128
    for j in range(64):
        p[c + j, j] = 1.0
        p[c + j, 64 + j] = 1.0
        p[s + j, 128 + j] = -1.0
        p[s + j, 128 + 64 + j] = 1.0
    for j in range(64):
        p[c + 127, 256 + j] = 1.0
    for j in range(16):
        cj, sj = c + 64 + j, s + 64 + j
        q0 = 256 + 64
        p[cj, q0 + j] = 1.0
        p[sj, q0 + 16 + j] = 1.0
        p[sj, q0 + 32 + j] = 1.0
        p[cj, q0 + 48 + j] = 1.0
        k0 = 384
        p[cj, k0 + j] = 1.0
        p[cj, k0 + 16 + j] = -1.0
        p[sj, k0 + 32 + j] = 1.0
        p[sj, k0 + 48 + j] = 1.0
        p[sj, k0 + 64 + j] = -1.0
        p[sj, k0 + 80 + j] = 1.0
        p[cj, k0 + 96 + j] = 1.0
        p[cj, k0 + 112 + j] = 1.0
    return p


def rope_tables(positions):
    T = positions.size
    tm = 512
    inv_ret = 1.0 / (ROPE_BASE ** (jnp.arange(0, RET_DK, 2, dtype=F32) / RET_DK))
    inv_mla = 1.0 / (ROPE_BASE ** (jnp.arange(0, MLA_ROPE, 2, dtype=F32) / MLA_ROPE))
    freq = jnp.concatenate([inv_ret, inv_mla, jnp.zeros((48,), F32)])[None, :]
    pos = positions.reshape(T, 1).astype(F32)
    perm = jnp.asarray(_table_perm(), BF16)
    return pl.pallas_call(
        _tables_kernel,
        grid=(T // tm,),
        in_specs=[pl.BlockSpec((tm, 1), lambda i: (i, 0)),
                  pl.BlockSpec((1, 128), lambda i: (0, 0)),
                  pl.BlockSpec((256, TAB_COLS), lambda i: (0, 0))],
        out_specs=pl.BlockSpec((tm, TAB_COLS), lambda i: (i, 0)),
        out_shape=jax.ShapeDtypeStruct((T, TAB_COLS), F32),
        compiler_params=_cp("parallel"),
        name="rope_tables",
    )(pos, freq, perm)


def _norm_matmul_kernel(x_ref, g_ref, w_ref, o_ref, hn_ref):
    @pl.when(pl.program_id(1) == 0)
    def _():
        x = x_ref[...]
        ms = jnp.mean(x * x, axis=-1, keepdims=True)
        hn_ref[...] = (x * lax.rsqrt(ms + NORM_EPS) * g_ref[...]).astype(BF16)

    o_ref[...] = _dot(hn_ref[...], w_ref[...]).astype(o_ref.dtype)


def norm_in_proj(h, g, w):
    T = h.shape[0]
    n = w.shape[1]
    tm, tn = 1024, 1536
    return pl.pallas_call(
        _norm_matmul_kernel,
        grid=(T // tm, n // tn),
        in_specs=[pl.BlockSpec((tm, D_MODEL), lambda i, j: (i, 0)),
                  pl.BlockSpec((1, D_MODEL), lambda i, j: (0, 0)),
                  pl.BlockSpec((D_MODEL, tn), lambda i, j: (0, j))],
        out_specs=pl.BlockSpec((tm, tn), lambda i, j: (i, j)),
        out_shape=jax.ShapeDtypeStruct((T, n), BF16),
        scratch_shapes=[pltpu.VMEM((tm, D_MODEL), BF16)],
        compiler_params=_cp("parallel", "arbitrary"),
        name="norm_in_proj",
    )(h, g, w)


def relayout_w_in(w):
    d = w.shape[0]
    kpe = 3872 + 256 + 128
    return jnp.concatenate([
        w[:, :3872],
        jnp.zeros((d, P_GATE - 3872), w.dtype),
        w[:, 4288:7360],
        w[:, 3872:4256],
        jnp.tile(w[:, kpe:kpe + 16], (1, 4)),
        jnp.tile(w[:, kpe + 16:kpe + 32], (1, 4)),
    ], axis=1).astype(BF16)


def _retention_kernel(q_ref, k_ref, v_ref, g_ref, cos_ref, sin_ref, mask_ref, qd_ref, kd_ref, cd_ref,
                      o_ref, st_ref):
    @pl.when(pl.program_id(1) == 0)
    def _():
        st_ref[...] = jnp.zeros_like(st_ref)

    cos2 = cos_ref[...]
    sin2 = sin_ref[...]
    for h in range(RET_HEADS):
        sl = slice(h * RET_DK, (h + 1) * RET_DK)
        q = q_ref[:, sl].astype(F32)
        k = k_ref[:, sl].astype(F32)
        q = q * cos2 + pltpu.roll(q, 64, 1) * sin2
        k = (k * cos2 + pltpu.roll(k, 64, 1) * sin2) * (RET_DK ** -0.5)
        v = v_ref[:, sl]
        qb = q.astype(BF16)
        kb = k.astype(BF16)
        scores = _dot_nt(qb, kb) * mask_ref[h]
        o = _dot(scores.astype(BF16), v)
        st = st_ref[h]
        o = o + _dot((q * qd_ref[h]).astype(BF16), st.astype(BF16))
        st_ref[h] = st * cd_ref[h] + _dot_tn((k * kd_ref[h]).astype(BF16), v)
        mean = jnp.mean(o, axis=-1, keepdims=True)
        oc = o - mean
        var = jnp.mean(oc * oc, axis=-1, keepdims=True)
        y = oc * lax.rsqrt(var + RET_GN_EPS)
        g = g_ref[:, sl].astype(F32)
        o_ref[:, sl] = (y * (g * _sigmoid(g))).astype(o_ref.dtype)


def _retention_consts():
    hh = jnp.arange(RET_HEADS, dtype=F32)
    log_gamma = jnp.log(1.0 - 2.0 ** (-5.0 - hh))
    idx = jnp.arange(BLK, dtype=F32)
    dist = jnp.abs(idx[:, None] - idx[None, :])
    ci = np.arange(BLK) // CHUNK
    visible = jnp.asarray(ci[None, :] <= ci[:, None])
    mask = jnp.where(visible[None], jnp.exp(log_gamma[:, None, None] * dist[None]), 0.0)
    qd = jnp.exp(log_gamma[:, None] * (idx[None, :] + 1.0))
    kd = jnp.exp(log_gamma[:, None] * (BLK - 1.0 - idx[None, :]))
    cd = jnp.exp(log_gamma * BLK)
    qd = jnp.broadcast_to(qd[:, :, None], (RET_HEADS, BLK, RET_DK))
    kd = jnp.broadcast_to(kd[:, :, None], (RET_HEADS, BLK, RET_DK))
    cd = jnp.broadcast_to(cd[:, None, None], (RET_HEADS, 1, RET_DK))
    return mask, qd, kd, cd


def retention(p, tab, B, S):
    T = B * S
    nb = S // BLK
    mask, qd, kd, cd = _retention_consts()
    row = lambda c: pl.BlockSpec((BLK, 512), lambda b, i, c=c: (b * nb + i, c))
    tcol = lambda c: pl.BlockSpec((BLK, 128), lambda b, i, c=c: (b * nb + i, c))
    const3 = lambda shape: pl.BlockSpec(shape, lambda b, i: (0, 0, 0))
    return pl.pallas_call(
        _retention_kernel,
        grid=(B, nb),
        in_specs=[row(0), row(1), row(2), row(3), tcol(0), tcol(1),
                  const3((RET_HEADS, BLK, BLK)), const3((RET_HEADS, BLK, RET_DK)),
                  const3((RET_HEADS, BLK, RET_DK)), const3((RET_HEADS, 1, RET_DK))],
        out_specs=pl.BlockSpec((BLK, 512), lambda b, i: (b * nb + i, 0)),
        out_shape=jax.ShapeDtypeStruct((T, 512), BF16),
        scratch_shapes=[pltpu.VMEM((RET_HEADS, RET_DK, RET_DK), F32)],
        compiler_params=_cp("arbitrary", "arbitrary"),
        name="retention",
    )(p, p, p, p, tab, tab, mask, qd, kd, cd)


def _rwkv_pre_kernel(p_ref, mu_ref, w0_ref, w2_ref, a0_ref, a2_ref, g2_ref, kk_w_ref, ka_ref, rk_ref, seg_ref,
                     r_o, w_o, k_o, v_o, kk_o, b_o, bonus_o, g_o, prev_ref):
    tm = p_ref.shape[0]

    @pl.when(pl.program_id(1) == 0)
    def _():
        prev_ref[...] = jnp.zeros_like(prev_ref)

    x = p_ref[...].astype(F32)
    row = lax.broadcasted_iota(jnp.int32, x.shape, 0)
    shifted = jnp.where(row == 0, prev_ref[7:8, :], pltpu.roll(x, 1, 0))
    prev_ref[...] = x[tm - 8:tm, :]
    xs = x + (shifted - x) * mu_ref[...]
    r = xs[:, 0:512]
    k = xs[:, 512:1024]
    v = xs[:, 1024:1536]
    xwa = xs[:, 1536:1664]
    xg = xs[:, 1664:1920]
    z = w0_ref[...] + _dot(jnp.tanh(xwa).astype(BF16), w2_ref[...])
    logw = -_sigmoid(z) * float(np.exp(-0.5))
    a = _sigmoid(a0_ref[...] + _dot(xwa.astype(BF16), a2_ref[...]))
    g = _dot(_sigmoid(xg).astype(BF16), g2_ref[...])
    seg = seg_ref[...]
    kk = k * kk_w_ref[...]
    kk = kk * lax.rsqrt(_dot_exact_rhs(kk * kk, seg) + 1e-12)
    k = k * (1.0 + (a - 1.0) * ka_ref[...])
    bonus = _dot_exact_rhs(r * k * rk_ref[...], seg) * v
    r_o[...] = r.astype(r_o.dtype)
    w_o[...] = logw
    k_o[...] = k.astype(k_o.dtype)
    v_o[...] = v.astype(v_o.dtype)
    kk_o[...] = kk.astype(kk_o.dtype)
    b_o[...] = (kk * a).astype(b_o.dtype)
    bonus_o[...] = bonus.astype(bonus_o.dtype)
    g_o[...] = g.astype(g_o.dtype)


def _rwkv_rec_kernel(r_ref, w_ref, k_ref, v_ref, kk_ref, b_ref, bonus_ref, g_ref, lnx_ref, o_ref, st_ref, y_ref):
    @pl.when(pl.program_id(2) == 0)
    def _():
        st_ref[...] = jnp.zeros_like(st_ref)

    r = r_ref[...].astype(F32)
    k = k_ref[...].astype(F32)
    kk = kk_ref[...].astype(F32)
    b = b_ref[...].astype(F32)
    vb = v_ref[...]
    logw = w_ref[...]

    row = lax.broadcasted_iota(jnp.int32, (BLK, BLK), 0)
    col = lax.broadcasted_iota(jnp.int32, (BLK, BLK), 1)
    same = (row >> 6) == (col >> 6)
    incl = same & (col <= row)
    strict = same & (col < row)
    cum = _dot_exact_lhs(jnp.where(incl, 1.0, 0.0).astype(BF16), logw)
    tot = _dot_exact_lhs(jnp.where(same, 1.0, 0.0).astype(BF16), logw)
    rt = r * jnp.exp(cum)
    kkt = kk * jnp.exp(cum - logw)
    e_neg = jnp.exp(-cum)
    kt = k * e_neg
    bt = b * e_neg
    e_end = jnp.exp(tot - cum)
    bh = b * e_end
    kh = k * e_end
    wc = jnp.exp(tot)

    lane = lax.broadcasted_iota(jnp.int32, (BLK, 128), 1)
    head0 = lane < RWKV_HEAD
    yt = jnp.concatenate([bt, kt], axis=0).astype(BF16)
    xs, lrb, lrk = [], [], []
    for h in range(2):
        mh = head0 if h == 0 else jnp.logical_not(head0)
        xh = jnp.concatenate([jnp.where(mh, kkt, 0.0), jnp.where(mh, rt, 0.0)], axis=0).astype(BF16)
        gm = _dot_nt(xh, yt)
        a_ub = jnp.where(strict, gm[:BLK, :BLK], 0.0)
        a_uk = jnp.where(strict, gm[:BLK, BLK:], 0.0)
        lrb.append(jnp.where(incl, gm[BLK:, :BLK], 0.0).astype(BF16))
        lrk.append(jnp.where(incl, gm[BLK:, BLK:], 0.0).astype(BF16))
        av = _dot(a_uk.astype(BF16), vb)
        x = jnp.concatenate([kkt, av], axis=1)
        apow = a_ub
        x = x - _dot(apow.astype(BF16), x.astype(BF16))
        for _ in range(5):
            apb = apow.astype(BF16)
            apow = _dot(apb, apb)
            x = x + _dot(apow.astype(BF16), x.astype(BF16))
        xs.append(x)
    pm = jnp.where(head0, xs[0][:, :128], xs[1][:, :128])
    qm = jnp.where(head0, xs[0][:, 128:], xs[1][:, 128:])
    pq = jnp.concatenate([pm, qm], axis=1).astype(BF16)
    lp0 = _dot(lrb[0], pq)
    lp1 = _dot(lrb[1], pq)
    rp = rt - jnp.where(head0, lp0[:, :128], lp1[:, :128])
    y0 = jnp.where(head0, _dot(lrk[0], vb) - lp0[:, 128:], _dot(lrk[1], vb) - lp1[:, 128:])

    rk_ = lax.broadcasted_iota(jnp.int32, (128, 128), 0)
    ck_ = lax.broadcasted_iota(jnp.int32, (128, 128), 1)
    bd = (rk_ >> 6) == (ck_ >> 6)
    st = st_ref[...]
    for c in range(BLK // CHUNK):
        sl = slice(c * CHUNK, (c + 1) * CHUNK)
        lhs = jnp.concatenate([pm[sl], qm[sl], vb[sl].astype(F32)], axis=1).astype(BF16)
        rhs = jnp.concatenate([bh[sl], kh[sl]], axis=1).astype(BF16)
        zz = _dot_tn(lhs, rhs)
        ptb = jnp.where(bd, zz[0:128, 0:128], 0.0)
        nn = jnp.where(bd, zz[256:384, 128:256] - zz[128:256, 0:128], 0.0)
        stb = st.astype(BF16)
        y_ref[sl, :] = _dot_nt(rp[sl].astype(BF16), stb) + y0[sl]
        st = st * wc[c * CHUNK:c * CHUNK + 1, :] - _dot(stb, ptb.astype(BF16)) + nn
    st_ref[...] = st

    y = y_ref[...]
    segm = jnp.where((lax.broadcasted_iota(jnp.int32, (128, 128), 0) >> 6) ==
                     (lax.broadcasted_iota(jnp.int32, (128, 128), 1) >> 6), 1.0 / RWKV_HEAD, 0.0).astype(BF16)
    mean = _dot_exact_rhs(y, segm)
    yc = y - mean
    var = _dot_exact_rhs(yc * yc, segm)
    yn = yc * lax.rsqrt(var + RWKV_GN_EPS) * lnx_ref[...]
    o_ref[...] = ((yn + bonus_ref[...].astype(F32)) * g_ref[...].astype(F32)).astype(o_ref.dtype)


def rwkv7(p, prm, B, S):
    T = B * S
    tm = 256
    nb = S // tm
    pad1 = lambda a, n: jnp.pad(a.astype(F32), (0, n - a.shape[0]))[None, :]
    mu = pad1(prm['rwkv_mu'], 2048)
    w2 = jnp.zeros((128, RWKV_W), F32).at[0:64].set(prm['rwkv_w2']).astype(BF16)
    a2 = jnp.zeros((128, RWKV_W), F32).at[64:128].set(prm['rwkv_a2']).astype(BF16)
    g2 = jnp.zeros((256, RWKV_W), F32).at[0:160].set(prm['rwkv_g2']).astype(BF16)
    hid = np.arange(RWKV_W) // RWKV_HEAD
    seg = jnp.asarray(hid[:, None] == hid[None, :], BF16)
    row1 = lambda a: a.reshape(1, RWKV_W).astype(F32)
    vec = pl.BlockSpec((1, RWKV_W), lambda b, i: (0, 0))
    out_blk = pl.BlockSpec((tm, RWKV_W), lambda b, i: (b * nb + i, 0))
    sds = lambda dt: jax.ShapeDtypeStruct((T, RWKV_W), dt)
    r, logw, k, v, kk, bb, bonus, g = pl.pallas_call(
        _rwkv_pre_kernel,
        grid=(B, nb),
        in_specs=[pl.BlockSpec((tm, 2048), lambda b, i: (b * nb + i, 1)),
                  pl.BlockSpec((1, 2048), lambda b, i: (0, 0)),
                  vec, pl.BlockSpec((128, RWKV_W), lambda b, i: (0, 0)),
                  vec, pl.BlockSpec((128, RWKV_W), lambda b, i: (0, 0)),
                  pl.BlockSpec((256, RWKV_W), lambda b, i: (0, 0)),
                  vec, vec, vec,
                  pl.BlockSpec((RWKV_W, RWKV_W), lambda b, i: (0, 0))],
        out_specs=[out_blk] * 8,
        out_shape=[sds(BF16), sds(F32), sds(BF16), sds(BF16), sds(BF16), sds(BF16), sds(BF16), sds(BF16)],
        scratch_shapes=[pltpu.VMEM((8, 2048), F32)],
        compiler_params=_cp("arbitrary", "arbitrary"),
        name="rwkv_pre",
    )(p, mu, row1(prm['rwkv_w0']), w2, row1(prm['rwkv_a0']), a2, g2,
      row1(prm['rwkv_k_k']), row1(prm['rwkv_k_a']), row1(prm['rwkv_r_k']), seg)

    nblk = S // BLK
    blk = pl.BlockSpec((BLK, 128), lambda b, j, i: (b * nblk + i, j))
    return pl.pallas_call(
        _rwkv_rec_kernel,
        grid=(B, RWKV_HEADS // 2, nblk),
        in_specs=[blk] * 8 + [pl.BlockSpec((1, 128), lambda b, j, i: (0, j))],
        out_specs=blk,
        out_shape=jax.ShapeDtypeStruct((T, RWKV_W), BF16),
        scratch_shapes=[pltpu.VMEM((128, 128), F32), pltpu.VMEM((BLK, 128), F32)],
        compiler_params=_cp("arbitrary", "arbitrary", "arbitrary"),
        name="rwkv_rec",
    )(r, logw, k, v, kk, bb, bonus, g, row1(prm['rwkv_lnx_w']))


MLA_TQ = 512


def _mla_prep_kernel(p_ref, tq_ref, tk_ref, qn_ref, wq_ref, kn_ref, wk_ref, wv_ref, q_o, k_o, v_o):
    x = p_ref[...].astype(F32)
    qa = x[:, 0:256]
    ckv = x[:, 256:384]
    kpe = x[:, 384:512]
    qn = (qa * lax.rsqrt(jnp.mean(qa * qa, axis=-1, keepdims=True) + NORM_EPS) * qn_ref[...]).astype(BF16)
    cn = (ckv * lax.rsqrt(jnp.mean(ckv * ckv, axis=-1, keepdims=True) + NORM_EPS) * kn_ref[...]).astype(BF16)
    q = _dot(qn, wq_ref[...])
    kn = _dot(cn, wk_ref[...])
    v_o[...] = _dot(cn, wv_ref[...]).astype(v_o.dtype)
    prod = kpe * tk_ref[...]
    lane = lax.broadcasted_iota(jnp.int32, prod.shape, 1)
    kr = jnp.where(lane >= 64, prod + pltpu.roll(prod, 64, 1), 0.0)
    tq = tq_ref[...]
    for h in range(MLA_HEADS):
        sl = slice(h * 128, (h + 1) * 128)
        q_o[:, sl] = (q[:, sl] * tq).astype(q_o.dtype)
        k_o[:, sl] = (kn[:, sl] + kr).astype(k_o.dtype)


def _mla_flash_kernel(q_ref, k_ref, v_ref, o_ref):
    i = pl.program_id(2)
    tq = MLA_TQ
    lane = lax.broadcasted_iota(jnp.int32, (tq, 128), 1)
    rowc = lax.broadcasted_iota(jnp.int32, (tq, tq), 0) >> 6
    colc = lax.broadcasted_iota(jnp.int32, (tq, tq), 1) >> 6
    diag_mask = colc <= rowc
    outs = []
    for h in range(2):
        hs = slice(h * 128, (h + 1) * 128)
        q = q_ref[:, hs]

        def step(s, v, carry):
            m, l, acc = carry
            m_new = jnp.maximum(m, jnp.max(s, axis=-1, keepdims=True))
            alpha = jnp.exp(m - m_new)
            pexp = jnp.exp(s - m_new)
            l = alpha * l + jnp.sum(pexp, axis=-1, keepdims=True)
            acc = alpha * acc + _dot(pexp.astype(BF16), v)
            return m_new, l, acc

        def body(j, carry):
            off = pl.multiple_of(j * tq, tq)
            s = _dot_nt(q, k_ref[pl.ds(off, tq), hs])
            return step(s, v_ref[pl.ds(off, tq), :], carry)

        init = (jnp.full((tq, 1), -1e30, F32), jnp.zeros((tq, 1), F32), jnp.zeros((tq, 128), F32))
        carry = lax.fori_loop(0, i, body, init)
        off = pl.multiple_of(i * tq, tq)
        s = jnp.where(diag_mask, _dot_nt(q, k_ref[pl.ds(off, tq), hs]), -1e30)
        m, l, acc = step(s, v_ref[pl.ds(off, tq), :], carry)
        outs.append(acc / l)
    o_ref[...] = jnp.where(lane < MLA_V, outs[0], outs[1]).astype(o_ref.dtype)


def mla(p, tab, q_norm, w_qb, kv_norm, w_kvb, B, S):
    T = B * S
    tm = 512
    scale = (MLA_NOPE + MLA_ROPE) ** -0.5
    wq3 = (w_qb * scale).reshape(MLA_Q_LORA, MLA_HEADS, MLA_NOPE + MLA_ROPE)
    x1 = wq3[:, :, MLA_NOPE:MLA_NOPE + 16]
    x2 = wq3[:, :, MLA_NOPE + 16:]
    wq = jnp.concatenate([wq3[:, :, :MLA_NOPE], x1, x2, x1, x2], axis=2).reshape(MLA_Q_LORA, 1024).astype(BF16)
    wkv3 = w_kvb.reshape(MLA_KV_LORA, MLA_HEADS, MLA_NOPE + MLA_V)
    wk = jnp.concatenate([wkv3[:, :, :MLA_NOPE], jnp.zeros((MLA_KV_LORA, MLA_HEADS, 64), w_kvb.dtype)],
                         axis=2).reshape(MLA_KV_LORA, 1024).astype(BF16)
    wv = wkv3[:, :, MLA_NOPE:].reshape(MLA_KV_LORA, 512).astype(BF16)
    full = lambda shape: pl.BlockSpec(shape, lambda i: (0, 0))
    q, k, v = pl.pallas_call(
        _mla_prep_kernel,
        grid=(T // tm,),
        in_specs=[pl.BlockSpec((tm, 512), lambda i: (i, P_MLA // 512)),
                  pl.BlockSpec((tm, 128), lambda i: (i, 2)),
                  pl.BlockSpec((tm, 128), lambda i: (i, 3)),
                  full((1, 256)), full((256, 1024)), full((1, 128)), full((128, 1024)), full((128, 512))],
        out_specs=[pl.BlockSpec((tm, 1024), lambda i: (i, 0)),
                   pl.BlockSpec((tm, 1024), lambda i: (i, 0)),
                   pl.BlockSpec((tm, 512), lambda i: (i, 0))],
        out_shape=[jax.ShapeDtypeStruct((T, 1024), BF16), jax.ShapeDtypeStruct((T, 1024), BF16),
                   jax.ShapeDtypeStruct((T, 512), BF16)],
        compiler_params=_cp("parallel"),
        name="mla_prep",
    )(p, tab, tab, q_norm[None, :].astype(F32), wq, kv_norm[None, :].astype(F32), wk, wv)

    nq = S // MLA_TQ
    return pl.pallas_call(
        _mla_flash_kernel,
        grid=(B, MLA_HEADS // 2, nq),
        in_specs=[pl.BlockSpec((MLA_TQ, 256), lambda b, j, i: (b * nq + i, j)),
                  pl.BlockSpec((S, 256), lambda b, j, i: (b, j)),
                  pl.BlockSpec((S, 128), lambda b, j, i: (b, j))],
        out_specs=pl.BlockSpec((MLA_TQ, 128), lambda b, j, i: (b * nq + i, j)),
        out_shape=jax.ShapeDtypeStruct((T, 512), BF16),
        compiler_params=_cp("parallel", "parallel", "arbitrary"),
        name="mla_flash",
    )(q, k, v)


def _merge_kernel(yr_ref, yw_ref, ym_ref, g0_ref, g1_ref, g2_ref, h_ref, wb_ref, wo_ref, o_ref):
    merged = None
    for n, (y_ref, g_ref) in enumerate(((yr_ref, g0_ref), (yw_ref, g1_ref), (ym_ref, g2_ref))):
        term = _sigmoid(g_ref[...].astype(F32)) * _dot(y_ref[...], wb_ref[n])
        merged = term if merged is None else merged + term
    o_ref[...] = h_ref[...] + _dot(merged.astype(BF16), wo_ref[...])


def merge(y_ret, y_rwkv, y_mla, p, h, w_branch, w_o):
    T = h.shape[0]
    tm = 512
    yb = pl.BlockSpec((tm, 512), lambda i: (i, 0))
    gate = lambda n: pl.BlockSpec((tm, D_MODEL), lambda i, n=n: (i, P_GATE // D_MODEL + n))
    return pl.pallas_call(
        _merge_kernel,
        grid=(T // tm,),
        in_specs=[yb, yb, yb, gate(0), gate(1), gate(2),
                  pl.BlockSpec((tm, D_MODEL), lambda i: (i, 0)),
                  pl.BlockSpec((3, 512, D_MODEL), lambda i: (0, 0, 0)),
                  pl.BlockSpec((D_MODEL, D_MODEL), lambda i: (0, 0))],
        out_specs=pl.BlockSpec((tm, D_MODEL), lambda i: (i, 0)),
        out_shape=jax.ShapeDtypeStruct((T, D_MODEL), F32),
        compiler_params=_cp("parallel"),
        name="merge",
    )(y_ret, y_rwkv, y_mla, p, p, p, h, w_branch.astype(BF16), w_o.astype(BF16))


def _rms(x, g):
    return x * lax.rsqrt(jnp.mean(x * x, axis=-1, keepdims=True) + NORM_EPS) * g


FFN_TF = 256


def _ffn_kernel(h_ref, g_ref, wg_ref, wu_ref, wd_ref, o_ref):
    h = h_ref[...]
    hn = _rms(h, g_ref[...]).astype(BF16)
    acc = h
    for c in range(D_FF // FFN_TF):
        sl = slice(c * FFN_TF, (c + 1) * FFN_TF)
        a = _dot(hn, wg_ref[:, sl])
        u = _dot(hn, wu_ref[:, sl])
        acc = acc + _dot((a * _sigmoid(a) * u).astype(BF16), wd_ref[sl, :])
    o_ref[...] = acc


def dense_ffn(h, g, wg, wu, wd):
    T = h.shape[0]
    tm = 512
    full = lambda shape: pl.BlockSpec(shape, lambda i: (0, 0))
    return pl.pallas_call(
        _ffn_kernel,
        grid=(T // tm,),
        in_specs=[pl.BlockSpec((tm, D_MODEL), lambda i: (i, 0)), full((1, D_MODEL)),
                  full((D_MODEL, D_FF)), full((D_MODEL, D_FF)), full((D_FF, D_MODEL))],
        out_specs=pl.BlockSpec((tm, D_MODEL), lambda i: (i, 0)),
        out_shape=jax.ShapeDtypeStruct((T, D_MODEL), F32),
        compiler_params=_cp("parallel"),
        name="dense_ffn",
    )(h, g[None, :], wg.astype(BF16), wu.astype(BF16), wd.astype(BF16))


def _router_kernel(h_ref, g_ref, rhi_ref, rlo_ref, o_ref):
    hn = _rms(h_ref[...], g_ref[...])
    logits = None
    for part in _split3(hn):
        for r_ref in (rhi_ref, rlo_ref):
            t = _dot(part, r_ref[...])
            logits = t if logits is None else logits + t
    lane = lax.broadcasted_iota(jnp.int32, logits.shape, 1)
    neg = jnp.float32(-1e30)
    logits = jnp.where(lane < N_EXPERTS, logits, neg)
    m1 = jnp.max(logits, axis=-1, keepdims=True)
    i1 = jnp.min(jnp.where(logits == m1, lane, 128), axis=-1, keepdims=True)
    rest = jnp.where(lane == i1, neg, logits)
    m2 = jnp.max(rest, axis=-1, keepdims=True)
    i2 = jnp.min(jnp.where(rest == m2, lane, 128), axis=-1, keepdims=True)
    e2 = jnp.exp(m2 - m1)
    w1 = 1.0 / (1.0 + e2)
    w2 = e2 / (1.0 + e2)
    o_ref[...] = jnp.where(lane == i1, w1, 0.0) + jnp.where(lane == i2, w2, 0.0)


def router(h, g, w_router):
    T = h.shape[0]
    tm = 512
    wr = jnp.pad(w_router.astype(F32), ((0, 0), (0, 128 - N_EXPERTS)))
    rhi = wr.astype(BF16)
    rlo = (wr - rhi.astype(F32)).astype(BF16)
    full = lambda shape: pl.BlockSpec(shape, lambda i: (0, 0))
    return pl.pallas_call(
        _router_kernel,
        grid=(T // tm,),
        in_specs=[pl.BlockSpec((tm, D_MODEL), lambda i: (i, 0)), full((1, D_MODEL)),
                  full((D_MODEL, 128)), full((D_MODEL, 128))],
        out_specs=pl.BlockSpec((tm, 128), lambda i: (i, 0)),
        out_shape=jax.ShapeDtypeStruct((T, 128), F32),
        compiler_params=_cp("parallel"),
        name="router",
    )(h, g[None, :], rhi, rlo)


MOE_TF = 896


def _moe_dense_kernel(h_ref, g_ref, comb_ref, wg_ref, wu_ref, wd_ref, fin_ref, o_ref, hn_ref, acc_ref):
    e = pl.program_id(1)
    f = pl.program_id(2)

    @pl.when((e == 0) & (f == 0))
    def _():
        hn_ref[...] = _rms(h_ref[...], g_ref[...]).astype(BF16)
        acc_ref[...] = jnp.zeros_like(acc_ref)

    comb = comb_ref[...]
    lane = lax.broadcasted_iota(jnp.int32, comb.shape, 1)
    ce = jnp.sum(jnp.where(lane == e, comb, 0.0), axis=-1, keepdims=True)
    hn = hn_ref[...]
    a = _dot(hn, wg_ref[0])
    u = _dot(hn, wu_ref[0])
    acc_ref[...] += _dot((a * _sigmoid(a) * u * ce).astype(BF16), wd_ref[0])

    @pl.when((e == pl.num_programs(1) - 1) & (f == pl.num_programs(2) - 1))
    def _():
        o_ref[...] = _rms(h_ref[...] + acc_ref[...], fin_ref[...])


def moe_dense_final(h, g, comb, wg, wu, wd, final_g):
    T = h.shape[0]
    tm = 1024
    nf = D_FF_EXPERT // MOE_TF
    return pl.pallas_call(
        _moe_dense_kernel,
        grid=(T // tm, N_EXPERTS, nf),
        in_specs=[pl.BlockSpec((tm, D_MODEL), lambda i, e, f: (i, 0)),
                  pl.BlockSpec((1, D_MODEL), lambda i, e, f: (0, 0)),
                  pl.BlockSpec((tm, 128), lambda i, e, f: (i, 0)),
                  pl.BlockSpec((1, D_MODEL, MOE_TF), lambda i, e, f: (e, 0, f)),
                  pl.BlockSpec((1, D_MODEL, MOE_TF), lambda i, e, f: (e, 0, f)),
                  pl.BlockSpec((1, MOE_TF, D_MODEL), lambda i, e, f: (e, f, 0)),
                  pl.BlockSpec((1, D_MODEL), lambda i, e, f: (0, 0))],
        out_specs=pl.BlockSpec((tm, D_MODEL), lambda i, e, f: (i, 0)),
        out_shape=jax.ShapeDtypeStruct((T, D_MODEL), F32),
        scratch_shapes=[pltpu.VMEM((tm, D_MODEL), BF16), pltpu.VMEM((tm, D_MODEL), F32)],
        compiler_params=_cp("parallel", "arbitrary", "arbitrary"),
        name="moe_dense",
    )(h, g[None, :], comb, wg.astype(BF16), wu.astype(BF16), wd.astype(BF16), final_g[None, :])


def _final_norm_kernel(h_ref, g_ref, o_ref):
    o_ref[...] = _rms(h_ref[...], g_ref[...])


def final_norm(h, g):
    T = h.shape[0]
    tm = 1024
    return pl.pallas_call(
        _final_norm_kernel,
        grid=(T // tm,),
        in_specs=[pl.BlockSpec((tm, D_MODEL), lambda i: (i, 0)), pl.BlockSpec((1, D_MODEL), lambda i: (0, 0))],
        out_specs=pl.BlockSpec((tm, D_MODEL), lambda i: (i, 0)),
        out_shape=jax.ShapeDtypeStruct((T, D_MODEL), F32),
        compiler_params=_cp("parallel"),
        name="final_norm",
    )(h, g[None, :])


def kernel(x, positions, attn_norm, w_in, rwkv_mu, rwkv_w0, rwkv_w2, rwkv_a0, rwkv_a2, rwkv_g2, rwkv_k_k, rwkv_k_a,
           rwkv_r_k, rwkv_lnx_w, mla_q_norm, mla_w_qb, mla_kv_norm, mla_w_kvb, w_branch, w_o, ffn_norm, ffn_w_gate,
           ffn_w_up, ffn_w_down, moe_router, moe_w_gate, moe_w_up, moe_w_down, final_norm_g):
    B, S, D = x.shape
    T = B * S
    depth = attn_norm.shape[0]
    rwkv_all = dict(rwkv_mu=rwkv_mu, rwkv_w0=rwkv_w0, rwkv_w2=rwkv_w2, rwkv_a0=rwkv_a0, rwkv_a2=rwkv_a2,
                    rwkv_g2=rwkv_g2, rwkv_k_k=rwkv_k_k, rwkv_k_a=rwkv_k_a, rwkv_r_k=rwkv_r_k,
                    rwkv_lnx_w=rwkv_lnx_w)
    tab = rope_tables(positions)
    h = x.reshape(T, D)
    out = None
    for l in range(depth):
        p = norm_in_proj(h, attn_norm[l][None, :], relayout_w_in(w_in[l]))
        y_ret = retention(p, tab, B, S)
        y_rwkv = rwkv7(p, {k: v[l] for k, v in rwkv_all.items()}, B, S)
        y_mla = mla(p, tab, mla_q_norm[l], mla_w_qb[l], mla_kv_norm[l], mla_w_kvb[l], B, S)
        h = merge(y_ret, y_rwkv, y_mla, p, h, w_branch[l], w_o[l])
        last = l == depth - 1
        if l % 2 == 0:
            h = dense_ffn(h, ffn_norm[l], ffn_w_gate[l // 2], ffn_w_up[l // 2], ffn_w_down[l // 2])
            if last:
                out = final_norm(h, final_norm_g)
        else:
            comb = router(h, ffn_norm[l], moe_router[l // 2])
            out_l = moe_dense_final(h, ffn_norm[l], comb, moe_w_gate[l // 2], moe_w_up[l // 2], moe_w_down[l // 2],
                                    final_norm_g)
            if last:
                out = out_l
            else:
                raise NotImplementedError("a MoE layer that is not the last layer")
    return out.reshape(B, S, D)
```

```python
import functools

import numpy as np
import jax
import jax.numpy as jnp
from jax import lax
from jax.experimental import pallas as pl
from jax.experimental.pallas import tpu as pltpu

F32 = jnp.float32
BF16 = jnp.bfloat16

D_MODEL = 1024
CHUNK = 64
NORM_EPS = 1e-6
ROPE_BASE = 10000.0

RET_HEADS = 4
RET_DK = 128
RET_GN_EPS = 1e-5

RWKV_HEAD = 64
RWKV_HEADS = 8
RWKV_W = 512
RWKV_GN_EPS = 64e-5
RWKV_COLS = 1824

MLA_HEADS = 8
MLA_Q_LORA = 256
MLA_KV_LORA = 128
MLA_NOPE = 64
MLA_ROPE = 32
MLA_V = 64

N_EXPERTS = 8
D_FF = 2816
D_FF_EXPERT = 3584

P_RET = 0
P_RWKV = 2048
P_GATE = 4096
P_MLA = 7168
P_COLS = 7680

BLK = 256
VMEM_LIMIT_BYTES = 56 * 1024 * 1024


def _cp(*sem):
    return pltpu.CompilerParams(dimension_semantics=sem, vmem_limit_bytes=VMEM_LIMIT_BYTES)


def _sigmoid(z):
    return 1.0 / (1.0 + jnp.exp(-z))


def _split3(x):
    hi = x.astype(BF16)
    r1 = x - hi.astype(F32)
    mid = r1.astype(BF16)
    lo = (r1 - mid.astype(F32)).astype(BF16)
    return hi, mid, lo


def _dot(a, b):
    return jnp.dot(a, b, preferred_element_type=F32)


def _dot_nt(a, b):
    return lax.dot_general(a, b, (((1,), (1,)), ((), ())), preferred_element_type=F32)


def _dot_tn(a, b):
    return lax.dot_general(a, b, (((0,), (0,)), ((), ())), preferred_element_type=F32)


def _dot_exact_lhs(m_bf16, x_f32):
    hi, mid, lo = _split3(x_f32)
    return _dot(m_bf16, hi) + _dot(m_bf16, mid) + _dot(m_bf16, lo)


def _dot_exact_rhs(x_f32, m_bf16):
    hi, mid, lo = _split3(x_f32)
    return _dot(hi, m_bf16) + _dot(mid, m_bf16) + _dot(lo, m_bf16)


TAB_COLS = 768


def _tables_kernel(pos_ref, freq_ref, perm_ref, o_ref):
    ang = pos_ref[...] * freq_ref[...]
    cs = jnp.concatenate([jnp.cos(ang), jnp.sin(ang)], axis=1)
    o_ref[...] = _dot_exact_rhs(cs, perm_ref[...])


def _table_perm():
    p = np.zeros((256, TAB_COLS), np.float32)
    c, s = 0, 128
    for j in range(64):
        p[c + j, j] = 1.0
        p[c + j, 64 + j] = 1.0
        p[s + j, 128 + j] = -1.0
        p[s + j, 128 + 64 + j] = 1.0
    for j in range(64):
        p[c + 127, 256 + j] = 1.0
    for j in range(16):
        cj, sj = c + 64 + j, s + 64 + j
        q0 = 256 + 64
        p[cj, q0 + j] = 1.0
        p[sj, q0 + 16 + j] = 1.0
        p[sj, q0 + 32 + j] = 1.0
        p[cj, q0 + 48 + j] = 1.0
        k0 = 384
        p[cj, k0 + j] = 1.0
        p[cj, k0 + 16 + j] = -1.0
        p[sj, k0 + 32 + j] = 1.0
        p[sj, k0 + 48 + j] = 1.0
        p[sj, k0 + 64 + j] = -1.0
        p[sj, k0 + 80 + j] = 1.0
        p[cj, k0 + 96 + j] = 1.0
        p[cj, k0 + 112 + j] = 1.0
    return p


def rope_tables(positions):
    T = positions.size
    tm = 512
    inv_ret = 1.0 / (ROPE_BASE ** (jnp.arange(0, RET_DK, 2, dtype=F32) / RET_DK))
    inv_mla = 1.0 / (ROPE_BASE ** (jnp.arange(0, MLA_ROPE, 2, dtype=F32) / MLA_ROPE))
    freq = jnp.concatenate([inv_ret, inv_mla, jnp.zeros((48,), F32)])[None, :]
    pos = positions.reshape(T, 1).astype(F32)
    perm = jnp.asarray(_table_perm(), BF16)
    return pl.pallas_call(
        _tables_kernel,
        grid=(T // tm,),
        in_specs=[pl.BlockSpec((tm, 1), lambda i: (i, 0)),
                  pl.BlockSpec((1, 128), lambda i: (0, 0)),
                  pl.BlockSpec((256, TAB_COLS), lambda i: (0, 0))],
        out_specs=pl.BlockSpec((tm, TAB_COLS), lambda i: (i, 0)),
        out_shape=jax.ShapeDtypeStruct((T, TAB_COLS), F32),
        compiler_params=_cp("parallel"),
        name="rope_tables",
    )(pos, freq, perm)


def _norm_matmul_kernel(x_ref, g_ref, w_ref, o_ref, hn_ref):
    @pl.when(pl.program_id(1) == 0)
    def _():
        x = x_ref[...]
        ms = jnp.mean(x * x, axis=-1, keepdims=True)
        hn_ref[...] = (x * lax.rsqrt(ms + NORM_EPS) * g_ref[...]).astype(BF16)

    o_ref[...] = _dot(hn_ref[...], w_ref[...]).astype(o_ref.dtype)


def norm_in_proj(h, g, w):
    T = h.shape[0]
    n = w.shape[1]
    tm, tn = 1024, 1536
    return pl.pallas_call(
        _norm_matmul_kernel,
        grid=(T // tm, n // tn),
        in_specs=[pl.BlockSpec((tm, D_MODEL), lambda i, j: (i, 0)),
                  pl.BlockSpec((1, D_MODEL), lambda i, j: (0, 0)),
                  pl.BlockSpec((D_MODEL, tn), lambda i, j: (0, j))],
        out_specs=pl.BlockSpec((tm, tn), lambda i, j: (i, j)),
        out_shape=jax.ShapeDtypeStruct((T, n), BF16),
        scratch_shapes=[pltpu.VMEM((tm, D_MODEL), BF16)],
        compiler_params=_cp("parallel", "arbitrary"),
        name="norm_in_proj",
    )(h, g, w)


def relayout_w_in(w):
    d = w.shape[0]
    kpe = 3872 + 256 + 128
    return jnp.concatenate([
        w[:, :3872],
        jnp.zeros((d, P_GATE - 3872), w.dtype),
        w[:, 4288:7360],
        w[:, 3872:4256],
        jnp.tile(w[:, kpe:kpe + 16], (1, 4)),
        jnp.tile(w[:, kpe + 16:kpe + 32], (1, 4)),
    ], axis=1).astype(BF16)


def _retention_kernel(q_ref, k_ref, v_ref, g_ref, cos_ref, sin_ref, mask_ref, qd_ref, kd_ref, cd_ref,
                      o_ref, st_ref):
    @pl.when(pl.program_id(1) == 0)
    def _():
        st_ref[...] = jnp.zeros_like(st_ref)

    cos2 = cos_ref[...]
    sin2 = sin_ref[...]
    for h in range(RET_HEADS):
        sl = slice(h * RET_DK, (h + 1) * RET_DK)
        q = q_ref[:, sl].astype(F32)
        k = k_ref[:, sl].astype(F32)
        q = q * cos2 + pltpu.roll(q, 64, 1) * sin2
        k = (k * cos2 + pltpu.roll(k, 64, 1) * sin2) * (RET_DK ** -0.5)
        v = v_ref[:, sl]
        qb = q.astype(BF16)
        kb = k.astype(BF16)
        scores = _dot_nt(qb, kb) * mask_ref[h]
        o = _dot(scores.astype(BF16), v)
        st = st_ref[h]
        o = o + _dot((q * qd_ref[h]).astype(BF16), st.astype(BF16))
        st_ref[h] = st * cd_ref[h] + _dot_tn((k * kd_ref[h]).astype(BF16), v)
        mean = jnp.mean(o, axis=-1, keepdims=True)
        oc = o - mean
        var = jnp.mean(oc * oc, axis=-1, keepdims=True)
        y = oc * lax.rsqrt(var + RET_GN_EPS)
        g = g_ref[:, sl].astype(F32)
        o_ref[:, sl] = (y * (g * _sigmoid(g))).astype(o_ref.dtype)


def _retention_consts():
    hh = jnp.arange(RET_HEADS, dtype=F32)
    log_gamma = jnp.log(1.0 - 2.0 ** (-5.0 - hh))
    idx = jnp.arange(BLK, dtype=F32)
    dist = jnp.abs(idx[:, None] - idx[None, :])
    ci = np.arange(BLK) // CHUNK
    visible = jnp.asarray(ci[None, :] <= ci[:, None])
    mask = jnp.where(visible[None], jnp.exp(log_gamma[:, None, None] * dist[None]), 0.0)
    qd = jnp.exp(log_gamma[:, None] * (idx[None, :] + 1.0))
    kd = jnp.exp(log_gamma[:, None] * (BLK - 1.0 - idx[None, :]))
    cd = jnp.exp(log_gamma * BLK)
    qd = jnp.broadcast_to(qd[:, :, None], (RET_HEADS, BLK, RET_DK))
    kd = jnp.broadcast_to(kd[:, :, None], (RET_HEADS, BLK, RET_DK))
    cd = jnp.broadcast_to(cd[:, None, None], (RET_HEADS, 1, RET_DK))
    return mask, qd, kd, cd


def retention(p, tab, B, S):
    T = B * S
    nb = S // BLK
    mask, qd, kd, cd = _retention_consts()
    row = lambda c: pl.BlockSpec((BLK, 512), lambda b, i, c=c: (b * nb + i, c))
    tcol = lambda c: pl.BlockSpec((BLK, 128), lambda b, i, c=c: (b * nb + i, c))
    const3 = lambda shape: pl.BlockSpec(shape, lambda b, i: (0, 0, 0))
    return pl.pallas_call(
        _retention_kernel,
        grid=(B, nb),
        in_specs=[row(0), row(1), row(2), row(3), tcol(0), tcol(1),
                  const3((RET_HEADS, BLK, BLK)), const3((RET_HEADS, BLK, RET_DK)),
                  const3((RET_HEADS, BLK, RET_DK)), const3((RET_HEADS, 1, RET_DK))],
        out_specs=pl.BlockSpec((BLK, 512), lambda b, i: (b * nb + i, 0)),
        out_shape=jax.ShapeDtypeStruct((T, 512), BF16),
        scratch_shapes=[pltpu.VMEM((RET_HEADS, RET_DK, RET_DK), F32)],
        compiler_params=_cp("arbitrary", "arbitrary"),
        name="retention",
    )(p, p, p, p, tab, tab, mask, qd, kd, cd)


def _rwkv_pre_kernel(p_ref, mu_ref, w0_ref, w2_ref, a0_ref, a2_ref, g2_ref, kk_w_ref, ka_ref, rk_ref, seg_ref,
                     r_o, w_o, k_o, v_o, kk_o, b_o, bonus_o, g_o, prev_ref):
    tm = p_ref.shape[0]

    @pl.when(pl.program_id(1) == 0)
    def _():
        prev_ref[...] = jnp.zeros_like(prev_ref)

    x = p_ref[...].astype(F32)
    row = lax.broadcasted_iota(jnp.int32, x.shape, 0)
    shifted = jnp.where(row == 0, prev_ref[7:8, :], pltpu.roll(x, 1, 0))
    prev_ref[...] = x[tm - 8:tm, :]
    xs = x + (shifted - x) * mu_ref[...]
    r = xs[:, 0:512]
    k = xs[:, 512:1024]
    v = xs[:, 1024:1536]
    xwa = xs[:, 1536:1664]
    xg = xs[:, 1664:1920]
    z = w0_ref[...] + _dot(jnp.tanh(xwa).astype(BF16), w2_ref[...])
    logw = -_sigmoid(z) * float(np.exp(-0.5))
    a = _sigmoid(a0_ref[...] + _dot(xwa.astype(BF16), a2_ref[...]))
    g = _dot(_sigmoid(xg).astype(BF16), g2_ref[...])
    seg = seg_ref[...]
    kk = k * kk_w_ref[...]
    kk = kk * lax.rsqrt(_dot_exact_rhs(kk * kk, seg) + 1e-12)
    k = k * (1.0 + (a - 1.0) * ka_ref[...])
    bonus = _dot_exact_rhs(r * k * rk_ref[...], seg) * v
    r_o[...] = r.astype(r_o.dtype)
    w_o[...] = logw
    k_o[...] = k.astype(k_o.dtype)
    v_o[...] = v.astype(v_o.dtype)
    kk_o[...] = kk.astype(kk_o.dtype)
    b_o[...] = (kk * a).astype(b_o.dtype)
    bonus_o[...] = bonus.astype(bonus_o.dtype)
    g_o[...] = g.astype(g_o.dtype)


def _rwkv_rec_kernel(r_ref, w_ref, k_ref, v_ref, kk_ref, b_ref, bonus_ref, g_ref, lnx_ref, o_ref, st_ref, y_ref):
    @pl.when(pl.program_id(2) == 0)
    def _():
        st_ref[...] = jnp.zeros_like(st_ref)

    r = r_ref[...].astype(F32)
    k = k_ref[...].astype(F32)
    kk = kk_ref[...].astype(F32)
    b = b_ref[...].astype(F32)
    vb = v_ref[...]
    logw = w_ref[...]

    row = lax.broadcasted_iota(jnp.int32, (BLK, BLK), 0)
    col = lax.broadcasted_iota(jnp.int32, (BLK, BLK), 1)
    same = (row >> 6) == (col >> 6)
    incl = same & (col <= row)
    strict = same & (col < row)
    cum = _dot_exact_lhs(jnp.where(incl, 1.0, 0.0).astype(BF16), logw)
    tot = _dot_exact_lhs(jnp.where(same, 1.0, 0.0).astype(BF16), logw)
    rt = r * jnp.exp(cum)
    kkt = kk * jnp.exp(cum - logw)
    e_neg = jnp.exp(-cum)
    kt = k * e_neg
    bt = b * e_neg
    e_end = jnp.exp(tot - cum)
    bh = b * e_end
    kh = k * e_end
    wc = jnp.exp(tot)

    lane = lax.broadcasted_iota(jnp.int32, (BLK, 128), 1)
    head0 = lane < RWKV_HEAD
    yt = jnp.concatenate([bt, kt], axis=0).astype(BF16)
    xs, lrb, lrk = [], [], []
    for h in range(2):
        mh = head0 if h == 0 else jnp.logical_not(head0)
        xh = jnp.concatenate([jnp.where(mh, kkt, 0.0), jnp.where(mh, rt, 0.0)], axis=0).astype(BF16)
        gm = _dot_nt(xh, yt)
        a_ub = jnp.where(strict, gm[:BLK, :BLK], 0.0)
        a_uk = jnp.where(strict, gm[:BLK, BLK:], 0.0)
        lrb.append(jnp.where(incl, gm[BLK:, :BLK], 0.0).astype(BF16))
        lrk.append(jnp.where(incl, gm[BLK:, BLK:], 0.0).astype(BF16))
        av = _dot(a_uk.astype(BF16), vb)
        x = jnp.concatenate([kkt, av], axis=1)
        apow = a_ub
        x = x - _dot(apow.astype(BF16), x.astype(BF16))
        for _ in range(5):
            apb = apow.astype(BF16)
            apow = _dot(apb, apb)
            x = x + _dot(apow.astype(BF16), x.astype(BF16))
        xs.append(x)
    pm = jnp.where(head0, xs[0][:, :128], xs[1][:, :128])
    qm = jnp.where(head0, xs[0][:, 128:], xs[1][:, 128:])
    pq = jnp.concatenate([pm, qm], axis=1).astype(BF16)
    lp0 = _dot(lrb[0], pq)
    lp1 = _dot(lrb[1], pq)
    rp = rt - jnp.where(head0, lp0[:, :128], lp1[:, :128])
    y0 = jnp.where(head0, _dot(lrk[0], vb) - lp0[:, 128:], _dot(lrk[1], vb) - lp1[:, 128:])

    rk_ = lax.broadcasted_iota(jnp.int32, (128, 128), 0)
    ck_ = lax.broadcasted_iota(jnp.int32, (128, 128), 1)
    bd = (rk_ >> 6) == (ck_ >> 6)
    st = st_ref[...]
    for c in range(BLK // CHUNK):
        sl = slice(c * CHUNK, (c + 1) * CHUNK)
        lhs = jnp.concatenate([pm[sl], qm[sl], vb[sl].astype(F32)], axis=1).astype(BF16)
        rhs = jnp.concatenate([bh[sl], kh[sl]], axis=1).astype(BF16)
        zz = _dot_tn(lhs, rhs)
        ptb = jnp.where(bd, zz[0:128, 0:128], 0.0)
        nn = jnp.where(bd, zz[256:384, 128:256] - zz[128:256, 0:128], 0.0)
        stb = st.astype(BF16)
        y_ref[sl, :] = _dot_nt(rp[sl].astype(BF16), stb) + y0[sl]
        st = st * wc[c * CHUNK:c * CHUNK + 1, :] - _dot(stb, ptb.astype(BF16)) + nn
    st_ref[...] = st

    y = y_ref[...]
    segm = jnp.where((lax.broadcasted_iota(jnp.int32, (128, 128), 0) >> 6) ==
                     (lax.broadcasted_iota(jnp.int32, (128, 128), 1) >> 6), 1.0 / RWKV_HEAD, 0.0).astype(BF16)
    mean = _dot_exact_rhs(y, segm)
    yc = y - mean
    var = _dot_exact_rhs(yc * yc, segm)
    yn = yc * lax.rsqrt(var + RWKV_GN_EPS) * lnx_ref[...]
    o_ref[...] = ((yn + bonus_ref[...].astype(F32)) * g_ref[...].astype(F32)).astype(o_ref.dtype)


def rwkv7(p, prm, B, S):
    T = B * S
    tm = 256
    nb = S // tm
    pad1 = lambda a, n: jnp.pad(a.astype(F32), (0, n - a.shape[0]))[None, :]
    mu = pad1(prm['rwkv_mu'], 2048)
    w2 = jnp.zeros((128, RWKV_W), F32).at[0:64].set(prm['rwkv_w2']).astype(BF16)
    a2 = jnp.zeros((128, RWKV_W), F32).at[64:128].set(prm['rwkv_a2']).astype(BF16)
    g2 = jnp.zeros((256, RWKV_W), F32).at[0:160].set(prm['rwkv_g2']).astype(BF16)
    hid = np.arange(RWKV_W) // RWKV_HEAD
    seg = jnp.asarray(hid[:, None] == hid[None, :], BF16)
    row1 = lambda a: a.reshape(1, RWKV_W).astype(F32)
    vec = pl.BlockSpec((1, RWKV_W), lambda b, i: (0, 0))
    out_blk = pl.BlockSpec((tm, RWKV_W), lambda b, i: (b * nb + i, 0))
    sds = lambda dt: jax.ShapeDtypeStruct((T, RWKV_W), dt)
    r, logw, k, v, kk, bb, bonus, g = pl.pallas_call(
        _rwkv_pre_kernel,
        grid=(B, nb),
        in_specs=[pl.BlockSpec((tm, 2048), lambda b, i: (b * nb + i, 1)),
                  pl.BlockSpec((1, 2048), lambda b, i: (0, 0)),
                  vec, pl.BlockSpec((128, RWKV_W), lambda b, i: (0, 0)),
                  vec, pl.BlockSpec((128, RWKV_W), lambda b, i: (0, 0)),
                  pl.BlockSpec((256, RWKV_W), lambda b, i: (0, 0)),
                  vec, vec, vec,
                  pl.BlockSpec((RWKV_W, RWKV_W), lambda b, i: (0, 0))],
        out_specs=[out_blk] * 8,
        out_shape=[sds(BF16), sds(F32), sds(BF16), sds(BF16), sds(BF16), sds(BF16), sds(BF16), sds(BF16)],
        scratch_shapes=[pltpu.VMEM((8, 2048), F32)],
        compiler_params=_cp("arbitrary", "arbitrary"),
        name="rwkv_pre",
    )(p, mu, row1(prm['rwkv_w0']), w2, row1(prm['rwkv_a0']), a2, g2,
      row1(prm['rwkv_k_k']), row1(prm['rwkv_k_a']), row1(prm['rwkv_r_k']), seg)

    nblk = S // BLK
    blk = pl.BlockSpec((BLK, 128), lambda b, j, i: (b * nblk + i, j))
    return pl.pallas_call(
        _rwkv_rec_kernel,
        grid=(B, RWKV_HEADS // 2, nblk),
        in_specs=[blk] * 8 + [pl.BlockSpec((1, 128), lambda b, j, i: (0, j))],
        out_specs=blk,
        out_shape=jax.ShapeDtypeStruct((T, RWKV_W), BF16),
        scratch_shapes=[pltpu.VMEM((128, 128), F32), pltpu.VMEM((BLK, 128), F32)],
        compiler_params=_cp("arbitrary", "arbitrary", "arbitrary"),
        name="rwkv_rec",
    )(r, logw, k, v, kk, bb, bonus, g, row1(prm['rwkv_lnx_w']))


MLA_TQ = 512


def _mla_prep_kernel(p_ref, tq_ref, tk_ref, qn_ref, wq_ref, kn_ref, wk_ref, wv_ref, q_o, k_o, v_o):
    x = p_ref[...].astype(F32)
    qa = x[:, 0:256]
    ckv = x[:, 256:384]
    kpe = x[:, 384:512]
    qn = (qa * lax.rsqrt(jnp.mean(qa * qa, axis=-1, keepdims=True) + NORM_EPS) * qn_ref[...]).astype(BF16)
    cn = (ckv * lax.rsqrt(jnp.mean(ckv * ckv, axis=-1, keepdims=True) + NORM_EPS) * kn_ref[...]).astype(BF16)
    q = _dot(qn, wq_ref[...])
    kn = _dot(cn, wk_ref[...])
    v_o[...] = _dot(cn, wv_ref[...]).astype(v_o.dtype)
    prod = kpe * tk_ref[...]
    lane = lax.broadcasted_iota(jnp.int32, prod.shape, 1)
    kr = jnp.where(lane >= 64, prod + pltpu.roll(prod, 64, 1), 0.0)
    tq = tq_ref[...]
    for h in range(MLA_HEADS):
        sl = slice(h * 128, (h + 1) * 128)
        q_o[:, sl] = (q[:, sl] * tq).astype(q_o.dtype)
        k_o[:, sl] = (kn[:, sl] + kr).astype(k_o.dtype)


def _mla_flash_kernel(q_ref, k_ref, v_ref, o_ref):
    i = pl.program_id(2)
    tq = MLA_TQ
    lane = lax.broadcasted_iota(jnp.int32, (tq, 128), 1)
    rowc = lax.broadcasted_iota(jnp.int32, (tq, tq), 0) >> 6
    colc = lax.broadcasted_iota(jnp.int32, (tq, tq), 1) >> 6
    diag_mask = colc <= rowc
    outs = []
    for h in range(2):
        hs = slice(h * 128, (h + 1) * 128)
        q = q_ref[:, hs]

        def step(s, v, carry):
            m, l, acc = carry
            m_new = jnp.maximum(m, jnp.max(s, axis=-1, keepdims=True))
            alpha = jnp.exp(m - m_new)
            pexp = jnp.exp(s - m_new)
            l = alpha * l + jnp.sum(pexp, axis=-1, keepdims=True)
            acc = alpha * acc + _dot(pexp.astype(BF16), v)
            return m_new, l, acc

        def body(j, carry):
            off = pl.multiple_of(j * tq, tq)
            s = _dot_nt(q, k_ref[pl.ds(off, tq), hs])
            return step(s, v_ref[pl.ds(off, tq), :], carry)

        init = (jnp.full((tq, 1), -1e30, F32), jnp.zeros((tq, 1), F32), jnp.zeros((tq, 128), F32))
        carry = lax.fori_loop(0, i, body, init)
        off = pl.multiple_of(i * tq, tq)
        s = jnp.where(diag_mask, _dot_nt(q, k_ref[pl.ds(off, tq), hs]), -1e30)
        m, l, acc = step(s, v_ref[pl.ds(off, tq), :], carry)
        outs.append(acc / l)
    o_ref[...] = jnp.where(lane < MLA_V, outs[0], outs[1]).astype(o_ref.dtype)


def mla(p, tab, q_norm, w_qb, kv_norm, w_kvb, B, S):
    T = B * S
    tm = 512
    scale = (MLA_NOPE + MLA_ROPE) ** -0.5
    wq3 = (w_qb * scale).reshape(MLA_Q_LORA, MLA_HEADS, MLA_NOPE + MLA_ROPE)
    x1 = wq3[:, :, MLA_NOPE:MLA_NOPE + 16]
    x2 = wq3[:, :, MLA_NOPE + 16:]
    wq = jnp.concatenate([wq3[:, :, :MLA_NOPE], x1, x2, x1, x2], axis=2).reshape(MLA_Q_LORA, 1024).astype(BF16)
    wkv3 = w_kvb.reshape(MLA_KV_LORA, MLA_HEADS, MLA_NOPE + MLA_V)
    wk = jnp.concatenate([wkv3[:, :, :MLA_NOPE], jnp.zeros((MLA_KV_LORA, MLA_HEADS, 64), w_kvb.dtype)],
                         axis=2).reshape(MLA_KV_LORA, 1024).astype(BF16)
    wv = wkv3[:, :, MLA_NOPE:].reshape(MLA_KV_LORA, 512).astype(BF16)
    full = lambda shape: pl.BlockSpec(shape, lambda i: (0, 0))
    q, k, v = pl.pallas_call(
        _mla_prep_kernel,
        grid=(T // tm,),
        in_specs=[pl.BlockSpec((tm, 512), lambda i: (i, P_MLA // 512)),
                  pl.BlockSpec((tm, 128), lambda i: (i, 2)),
                  pl.BlockSpec((tm, 128), lambda i: (i, 3)),
                  full((1, 256)), full((256, 1024)), full((1, 128)), full((128, 1024)), full((128, 512))],
        out_specs=[pl.BlockSpec((tm, 1024), lambda i: (i, 0)),
                   pl.BlockSpec((tm, 1024), lambda i: (i, 0)),
                   pl.BlockSpec((tm, 512), lambda i: (i, 0))],
        out_shape=[jax.ShapeDtypeStruct((T, 1024), BF16), jax.ShapeDtypeStruct((T, 1024), BF16),
                   jax.ShapeDtypeStruct((T, 512), BF16)],
        compiler_params=_cp("parallel"),
        name="mla_prep",
    )(p, tab, tab, q_norm[None, :].astype(F32), wq, kv_norm[None, :].astype(F32), wk, wv)

    nq = S // MLA_TQ
    return pl.pallas_call(
        _mla_flash_kernel,
        grid=(B, MLA_HEADS // 2, nq),
        in_specs=[pl.BlockSpec((MLA_TQ, 256), lambda b, j, i: (b * nq + i, j)),
                  pl.BlockSpec((S, 256), lambda b, j, i: (b, j)),
                  pl.BlockSpec((S, 128), lambda b, j, i: (b, j))],
        out_specs=pl.BlockSpec((MLA_TQ, 128), lambda b, j, i: (b * nq + i, j)),
        out_shape=jax.ShapeDtypeStruct((T, 512), BF16),
        compiler_params=_cp("parallel", "parallel", "arbitrary"),
        name="mla_flash",
    )(q, k, v)


def _merge_kernel(yr_ref, yw_ref, ym_ref, g0_ref, g1_ref, g2_ref, h_ref, wb_ref, wo_ref, o_ref):
    merged = None
    for n, (y_ref, g_ref) in enumerate(((yr_ref, g0_ref), (yw_ref, g1_ref), (ym_ref, g2_ref))):
        term = _sigmoid(g_ref[...].astype(F32)) * _dot(y_ref[...], wb_ref[n])
        merged = term if merged is None else merged + term
    o_ref[...] = h_ref[...] + _dot(merged.astype(BF16), wo_ref[...])


def merge(y_ret, y_rwkv, y_mla, p, h, w_branch, w_o):
    T = h.shape[0]
    tm = 512
    yb = pl.BlockSpec((tm, 512), lambda i: (i, 0))
    gate = lambda n: pl.BlockSpec((tm, D_MODEL), lambda i, n=n: (i, P_GATE // D_MODEL + n))
    return pl.pallas_call(
        _merge_kernel,
        grid=(T // tm,),
        in_specs=[yb, yb, yb, gate(0), gate(1), gate(2),
                  pl.BlockSpec((tm, D_MODEL), lambda i: (i, 0)),
                  pl.BlockSpec((3, 512, D_MODEL), lambda i: (0, 0, 0)),
                  pl.BlockSpec((D_MODEL, D_MODEL), lambda i: (0, 0))],
        out_specs=pl.BlockSpec((tm, D_MODEL), lambda i: (i, 0)),
        out_shape=jax.ShapeDtypeStruct((T, D_MODEL), F32),
        compiler_params=_cp("parallel"),
        name="merge",
    )(y_ret, y_rwkv, y_mla, p, p, p, h, w_branch.astype(BF16), w_o.astype(BF16))


def _rms(x, g):
    return x * lax.rsqrt(jnp.mean(x * x, axis=-1, keepdims=True) + NORM_EPS) * g


FFN_TF = 256


def _ffn_kernel(h_ref, g_ref, wg_ref, wu_ref, wd_ref, o_ref):
    h = h_ref[...]
    hn = _rms(h, g_ref[...]).astype(BF16)
    acc = h
    for c in range(D_FF // FFN_TF):
        sl = slice(c * FFN_TF, (c + 1) * FFN_TF)
        a = _dot(hn, wg_ref[:, sl])
        u = _dot(hn, wu_ref[:, sl])
        acc = acc + _dot((a * _sigmoid(a) * u).astype(BF16), wd_ref[sl, :])
    o_ref[...] = acc


def dense_ffn(h, g, wg, wu, wd):
    T = h.shape[0]
    tm = 512
    full = lambda shape: pl.BlockSpec(shape, lambda i: (0, 0))
    return pl.pallas_call(
        _ffn_kernel,
        grid=(T // tm,),
        in_specs=[pl.BlockSpec((tm, D_MODEL), lambda i: (i, 0)), full((1, D_MODEL)),
                  full((D_MODEL, D_FF)), full((D_MODEL, D_FF)), full((D_FF, D_MODEL))],
        out_specs=pl.BlockSpec((tm, D_MODEL), lambda i: (i, 0)),
        out_shape=jax.ShapeDtypeStruct((T, D_MODEL), F32),
        compiler_params=_cp("parallel"),
        name="dense_ffn",
    )(h, g[None, :], wg.astype(BF16), wu.astype(BF16), wd.astype(BF16))


def _router_kernel(h_ref, g_ref, rhi_ref, rlo_ref, o_ref):
    hn = _rms(h_ref[...], g_ref[...])
    logits = None
    for part in _split3(hn):
        for r_ref in (rhi_ref, rlo_ref):
            t = _dot(part, r_ref[...])
            logits = t if logits is None else logits + t
    lane = lax.broadcasted_iota(jnp.int32, logits.shape, 1)
    neg = jnp.float32(-1e30)
    logits = jnp.where(lane < N_EXPERTS, logits, neg)
    m1 = jnp.max(logits, axis=-1, keepdims=True)
    i1 = jnp.min(jnp.where(logits == m1, lane, 128), axis=-1, keepdims=True)
    rest = jnp.where(lane == i1, neg, logits)
    m2 = jnp.max(rest, axis=-1, keepdims=True)
    i2 = jnp.min(jnp.where(rest == m2, lane, 128), axis=-1, keepdims=True)
    e2 = jnp.exp(m2 - m1)
    w1 = 1.0 / (1.0 + e2)
    w2 = e2 / (1.0 + e2)
    o_ref[...] = (jnp.where(lane == i1, w1, 0.0) + jnp.where(lane == i2, w2, 0.0)
                  + jnp.where(lane == 8, w1, 0.0) + jnp.where(lane == 9, w2, 0.0)
                  + jnp.where(lane == 10, i1.astype(F32), 0.0) + jnp.where(lane == 11, i2.astype(F32), 0.0))


def router(h, g, w_router):
    T = h.shape[0]
    tm = 512
    wr = jnp.pad(w_router.astype(F32), ((0, 0), (0, 128 - N_EXPERTS)))
    rhi = wr.astype(BF16)
    rlo = (wr - rhi.astype(F32)).astype(BF16)
    full = lambda shape: pl.BlockSpec(shape, lambda i: (0, 0))
    return pl.pallas_call(
        _router_kernel,
        grid=(T // tm,),
        in_specs=[pl.BlockSpec((tm, D_MODEL), lambda i: (i, 0)), full((1, D_MODEL)),
                  full((D_MODEL, 128)), full((D_MODEL, 128))],
        out_specs=pl.BlockSpec((tm, 128), lambda i: (i, 0)),
        out_shape=jax.ShapeDtypeStruct((T, 128), F32),
        compiler_params=_cp("parallel"),
        name="router",
    )(h, g[None, :], rhi, rlo)


MOE_TM = 512
MOE_TF = 1792
MOE_DISPATCH_TM = 512


def _row_copy(src_ref, src_row, dst_ref, dst_row, sem):
    return pltpu.make_async_copy(src_ref.at[pl.ds(src_row, 1)], dst_ref.at[pl.ds(dst_row, 1)], sem)


def _dispatch_kernel(pos1_ref, pos2_ref, h_ref, g_ref, xs_in_ref, xs_ref, hn_ref, sem):
    del xs_in_ref
    tm = h_ref.shape[0]
    base = pl.program_id(0) * tm
    hn_ref[...] = _rms(h_ref[...], g_ref[...])

    def issue(r, c):
        _row_copy(hn_ref, r, xs_ref, pos1_ref[base + r], sem).start()
        _row_copy(hn_ref, r, xs_ref, pos2_ref[base + r], sem).start()
        return c

    def drain(r, c):
        _row_copy(hn_ref, r, xs_ref, pos1_ref[base + r], sem).wait()
        _row_copy(hn_ref, r, xs_ref, pos2_ref[base + r], sem).wait()
        return c

    lax.fori_loop(0, tm, issue, 0)
    lax.fori_loop(0, tm, drain, 0)


def _moe_kernel(te_ref, nv_ref, xs_ref, wg_ref, wu_ref, wd_ref, y_ref, xb_ref, acc_ref):
    del te_ref
    i = pl.program_id(0)
    f = pl.program_id(1)

    @pl.when(i < nv_ref[0])
    def _():
        @pl.when(f == 0)
        def _():
            xb_ref[...] = xs_ref[...].astype(BF16)

        xb = xb_ref[...]
        a = _dot(xb, wg_ref[0])
        u = _dot(xb, wu_ref[0])
        t = _dot((a * _sigmoid(a) * u).astype(BF16), wd_ref[0])

        @pl.when(f == 0)
        def _():
            acc_ref[...] = t

        @pl.when(f > 0)
        def _():
            acc_ref[...] += t

        @pl.when(f == pl.num_programs(1) - 1)
        def _():
            y_ref[...] = acc_ref[...]

    @pl.when(i >= nv_ref[0])
    def _():
        y_ref[...] = jnp.zeros_like(y_ref)


def _combine_kernel(pos1_ref, pos2_ref, h_ref, rt_ref, fin_ref, y_ref, o_ref, ybuf_ref, sem):
    tm = h_ref.shape[0]
    base = pl.program_id(0) * tm

    def issue(r, c):
        _row_copy(y_ref, pos1_ref[base + r], ybuf_ref.at[0], r, sem).start()
        _row_copy(y_ref, pos2_ref[base + r], ybuf_ref.at[1], r, sem).start()
        return c

    def drain(r, c):
        _row_copy(y_ref, pos1_ref[base + r], ybuf_ref.at[0], r, sem).wait()
        _row_copy(y_ref, pos2_ref[base + r], ybuf_ref.at[1], r, sem).wait()
        return c

    lax.fori_loop(0, tm, issue, 0)
    lax.fori_loop(0, tm, drain, 0)
    rt = rt_ref[...]
    lane = lax.broadcasted_iota(jnp.int32, rt.shape, 1)
    w1 = jnp.sum(jnp.where(lane == 8, rt, 0.0), axis=-1, keepdims=True)
    w2 = jnp.sum(jnp.where(lane == 9, rt, 0.0), axis=-1, keepdims=True)
    o_ref[...] = _rms(h_ref[...] + (w1 * ybuf_ref[0] + w2 * ybuf_ref[1]), fin_ref[...])


def _moe_plan(rt, n_tiles):
    e1 = rt[:, 10].astype(jnp.int32)
    e2 = rt[:, 11].astype(jnp.int32)
    eid = jnp.arange(N_EXPERTS, dtype=jnp.int32)[None, :]
    m1 = (e1[:, None] == eid).astype(jnp.int32)
    m2 = (e2[:, None] == eid).astype(jnp.int32)
    m = m1 + m2
    counts = jnp.sum(m, axis=0)
    rank = jnp.cumsum(m, axis=0) - m
    padded = ((counts + MOE_TM - 1) // MOE_TM) * MOE_TM
    ends = jnp.cumsum(padded)
    pos_te = (ends - padded)[None, :] + rank
    pos1 = jnp.sum(m1 * pos_te, axis=1).astype(jnp.int32)
    pos2 = jnp.sum(m2 * pos_te, axis=1).astype(jnp.int32)
    nvalid = (ends[-1] // MOE_TM).astype(jnp.int32)
    starts = jnp.minimum(jnp.arange(n_tiles, dtype=jnp.int32), nvalid - 1) * MOE_TM
    tile_expert = jnp.sum((starts[:, None] >= ends[None, :]).astype(jnp.int32), axis=1)
    return pos1, pos2, tile_expert.astype(jnp.int32), nvalid.reshape(1)


def moe_final(h, g, rt, wg, wu, wd, final_g):
    T = h.shape[0]
    n_tiles = (2 * T) // MOE_TM + N_EXPERTS
    n_rows = n_tiles * MOE_TM
    nf = D_FF_EXPERT // MOE_TF
    pos1, pos2, tile_expert, nvalid = _moe_plan(rt, n_tiles)

    tm = MOE_DISPATCH_TM
    xs = pl.pallas_call(
        _dispatch_kernel,
        grid_spec=pltpu.PrefetchScalarGridSpec(
            num_scalar_prefetch=2,
            grid=(T // tm,),
            in_specs=[pl.BlockSpec((tm, D_MODEL), lambda i, p1, p2: (i, 0)),
                      pl.BlockSpec((1, D_MODEL), lambda i, p1, p2: (0, 0)),
                      pl.BlockSpec(memory_space=pl.ANY)],
            out_specs=pl.BlockSpec(memory_space=pl.ANY),
            scratch_shapes=[pltpu.VMEM((tm, D_MODEL), F32), pltpu.SemaphoreType.DMA(())]),
        out_shape=jax.ShapeDtypeStruct((n_rows, D_MODEL), F32),
        input_output_aliases={4: 0},
        compiler_params=_cp("arbitrary"),
        name="moe_dispatch",
    )(pos1, pos2, h, g[None, :], jnp.zeros((n_rows, D_MODEL), F32))

    def row_tile(i, f, te, nv):
        return (jnp.minimum(i, nv[0] - 1), 0)

    def w_up(i, f, te, nv):
        return (te[i], 0, jnp.where(i < nv[0], f, nf - 1))

    def w_down(i, f, te, nv):
        return (te[i], jnp.where(i < nv[0], f, nf - 1), 0)

    y = pl.pallas_call(
        _moe_kernel,
        grid_spec=pltpu.PrefetchScalarGridSpec(
            num_scalar_prefetch=2,
            grid=(n_tiles, nf),
            in_specs=[pl.BlockSpec((MOE_TM, D_MODEL), row_tile),
                      pl.BlockSpec((1, D_MODEL, MOE_TF), w_up),
                      pl.BlockSpec((1, D_MODEL, MOE_TF), w_up),
                      pl.BlockSpec((1, MOE_TF, D_MODEL), w_down)],
            out_specs=pl.BlockSpec((MOE_TM, D_MODEL), lambda i, f, te, nv: (i, 0)),
            scratch_shapes=[pltpu.VMEM((MOE_TM, D_MODEL), BF16), pltpu.VMEM((MOE_TM, D_MODEL), F32)]),
        out_shape=jax.ShapeDtypeStruct((n_rows, D_MODEL), F32),
        compiler_params=_cp("arbitrary", "arbitrary"),
        name="moe_experts",
    )(tile_expert, nvalid, xs, wg.astype(BF16), wu.astype(BF16), wd.astype(BF16))

    return pl.pallas_call(
        _combine_kernel,
        grid_spec=pltpu.PrefetchScalarGridSpec(
            num_scalar_prefetch=2,
            grid=(T // tm,),
            in_specs=[pl.BlockSpec((tm, D_MODEL), lambda i, p1, p2: (i, 0)),
                      pl.BlockSpec((tm, 128), lambda i, p1, p2: (i, 0)),
                      pl.BlockSpec((1, D_MODEL), lambda i, p1, p2: (0, 0)),
                      pl.BlockSpec(memory_space=pl.ANY)],
            out_specs=pl.BlockSpec((tm, D_MODEL), lambda i, p1, p2: (i, 0)),
            scratch_shapes=[pltpu.VMEM((2, tm, D_MODEL), F32), pltpu.SemaphoreType.DMA(())]),
        out_shape=jax.ShapeDtypeStruct((T, D_MODEL), F32),
        compiler_params=_cp("arbitrary"),
        name="moe_combine",
    )(pos1, pos2, h, rt, final_g[None, :], y)


def _final_norm_kernel(h_ref, g_ref, o_ref):
    o_ref[...] = _rms(h_ref[...], g_ref[...])


def final_norm(h, g):
    T = h.shape[0]
    tm = 1024
    return pl.pallas_call(
        _final_norm_kernel,
        grid=(T // tm,),
        in_specs=[pl.BlockSpec((tm, D_MODEL), lambda i: (i, 0)), pl.BlockSpec((1, D_MODEL), lambda i: (0, 0))],
        out_specs=pl.BlockSpec((tm, D_MODEL), lambda i: (i, 0)),
        out_shape=jax.ShapeDtypeStruct((T, D_MODEL), F32),
        compiler_params=_cp("parallel"),
        name="final_norm",
    )(h, g[None, :])


def kernel(x, positions, attn_norm, w_in, rwkv_mu, rwkv_w0, rwkv_w2, rwkv_a0, rwkv_a2, rwkv_g2, rwkv_k_k, rwkv_k_a,
           rwkv_r_k, rwkv_lnx_w, mla_q_norm, mla_w_qb, mla_kv_norm, mla_w_kvb, w_branch, w_o, ffn_norm, ffn_w_gate,
           ffn_w_up, ffn_w_down, moe_router, moe_w_gate, moe_w_up, moe_w_down, final_norm_g):
    B, S, D = x.shape
    T = B * S
    depth = attn_norm.shape[0]
    rwkv_all = dict(rwkv_mu=rwkv_mu, rwkv_w0=rwkv_w0, rwkv_w2=rwkv_w2, rwkv_a0=rwkv_a0, rwkv_a2=rwkv_a2,
                    rwkv_g2=rwkv_g2, rwkv_k_k=rwkv_k_k, rwkv_k_a=rwkv_k_a, rwkv_r_k=rwkv_r_k,
                    rwkv_lnx_w=rwkv_lnx_w)
    tab = rope_tables(positions)
    h = x.reshape(T, D)
    out = None
    for l in range(depth):
        p = norm_in_proj(h, attn_norm[l][None, :], relayout_w_in(w_in[l]))
        y_ret = retention(p, tab, B, S)
        y_rwkv = rwkv7(p, {k: v[l] for k, v in rwkv_all.items()}, B, S)
        y_mla = mla(p, tab, mla_q_norm[l], mla_w_qb[l], mla_kv_norm[l], mla_w_kvb[l], B, S)
        h = merge(y_ret, y_rwkv, y_mla, p, h, w_branch[l], w_o[l])
        last = l == depth - 1
        if l % 2 == 0:
            h = dense_ffn(h, ffn_norm[l], ffn_w_gate[l // 2], ffn_w_up[l // 2], ffn_w_down[l // 2])
            if last:
                out = final_norm(h, final_norm_g)
        else:
            rt = router(h, ffn_norm[l], moe_router[l // 2])
            out_l = moe_final(h, ffn_norm[l], rt, moe_w_gate[l // 2], moe_w_up[l // 2], moe_w_down[l // 2],
                              final_norm_g)
            if last:
                out = out_l
            else:
                raise NotImplementedError("a MoE layer that is not the last layer")
    return out.reshape(B, S, D)
```

```python
import functools

import numpy as np
import jax
import jax.numpy as jnp
from jax import lax
from jax.experimental import pallas as pl
from jax.experimental.pallas import tpu as pltpu

F32 = jnp.float32
BF16 = jnp.bfloat16

D_MODEL = 1024
CHUNK = 64
NORM_EPS = 1e-6
ROPE_BASE = 10000.0

RET_HEADS = 4
RET_DK = 128
RET_GN_EPS = 1e-5

RWKV_HEAD = 64
RWKV_HEADS = 8
RWKV_W = 512
RWKV_GN_EPS = 64e-5
RWKV_COLS = 1824

MLA_HEADS = 8
MLA_Q_LORA = 256
MLA_KV_LORA = 128
MLA_NOPE = 64
MLA_ROPE = 32
MLA_V = 64

N_EXPERTS = 8
D_FF = 2816
D_FF_EXPERT = 3584

P_RET = 0
P_RWKV = 2048
P_GATE = 4096
P_MLA = 7168
P_COLS = 7680

BLK = 256
VMEM_LIMIT_BYTES = 56 * 1024 * 1024


def _cp(*sem):
    return pltpu.CompilerParams(dimension_semantics=sem, vmem_limit_bytes=VMEM_LIMIT_BYTES)


def _sigmoid(z):
    return 1.0 / (1.0 + jnp.exp(-z))


def _split3(x):
    hi = x.astype(BF16)
    r1 = x - hi.astype(F32)
    mid = r1.astype(BF16)
    lo = (r1 - mid.astype(F32)).astype(BF16)
    return hi, mid, lo


def _dot(a, b):
    return jnp.dot(a, b, preferred_element_type=F32)


def _dot_nt(a, b):
    return lax.dot_general(a, b, (((1,), (1,)), ((), ())), preferred_element_type=F32)


def _dot_tn(a, b):
    return lax.dot_general(a, b, (((0,), (0,)), ((), ())), preferred_element_type=F32)


def _dot_exact_lhs(m_bf16, x_f32):
    hi, mid, lo = _split3(x_f32)
    return _dot(m_bf16, hi) + _dot(m_bf16, mid) + _dot(m_bf16, lo)


def _dot_exact_rhs(x_f32, m_bf16):
    hi, mid, lo = _split3(x_f32)
    return _dot(hi, m_bf16) + _dot(mid, m_bf16) + _dot(lo, m_bf16)


TAB_COLS = 768


def _tables_kernel(pos_ref, freq_ref, perm_ref, o_ref):
    ang = pos_ref[...] * freq_ref[...]
    cs = jnp.concatenate([jnp.cos(ang), jnp.sin(ang)], axis=1)
    o_ref[...] = _dot_exact_rhs(cs, perm_ref[...])


def _table_perm():
    p = np.zeros((256, TAB_COLS), np.float32)
    c, s = 0, 128
    for j in range(64):
        p[c + j, j] = 1.0
        p[c + j, 64 + j] = 1.0
        p[s + j, 128 + j] = -1.0
        p[s + j, 128 + 64 + j] = 1.0
    for j in range(64):
        p[c + 127, 256 + j] = 1.0
    for j in range(16):
        cj, sj = c + 64 + j, s + 64 + j
        q0 = 256 + 64
        p[cj, q0 + j] = 1.0
        p[sj, q0 + 16 + j] = 1.0
        p[sj, q0 + 32 + j] = 1.0
        p[cj, q0 + 48 + j] = 1.0
        k0 = 384
        p[cj, k0 + j] = 1.0
        p[cj, k0 + 16 + j] = -1.0
        p[sj, k0 + 32 + j] = 1.0
        p[sj, k0 + 48 + j] = 1.0
        p[sj, k0 + 64 + j] = -1.0
        p[sj, k0 + 80 + j] = 1.0
        p[cj, k0 + 96 + j] = 1.0
        p[cj, k0 + 112 + j] = 1.0
    return p


def rope_tables(positions):
    T = positions.size
    tm = 512
    inv_ret = 1.0 / (ROPE_BASE ** (jnp.arange(0, RET_DK, 2, dtype=F32) / RET_DK))
    inv_mla = 1.0 / (ROPE_BASE ** (jnp.arange(0, MLA_ROPE, 2, dtype=F32) / MLA_ROPE))
    freq = jnp.concatenate([inv_ret, inv_mla, jnp.zeros((48,), F32)])[None, :]
    pos = positions.reshape(T, 1).astype(F32)
    perm = jnp.asarray(_table_perm(), BF16)
    return pl.pallas_call(
        _tables_kernel,
        grid=(T // tm,),
        in_specs=[pl.BlockSpec((tm, 1), lambda i: (i, 0)),
                  pl.BlockSpec((1, 128), lambda i: (0, 0)),
                  pl.BlockSpec((256, TAB_COLS), lambda i: (0, 0))],
        out_specs=pl.BlockSpec((tm, TAB_COLS), lambda i: (i, 0)),
        out_shape=jax.ShapeDtypeStruct((T, TAB_COLS), F32),
        compiler_params=_cp("parallel"),
        name="rope_tables",
    )(pos, freq, perm)


def _norm_matmul_kernel(x_ref, g_ref, w_ref, o_ref, hn_ref):
    @pl.when(pl.program_id(1) == 0)
    def _():
        x = x_ref[...]
        ms = jnp.mean(x * x, axis=-1, keepdims=True)
        hn_ref[...] = (x * lax.rsqrt(ms + NORM_EPS) * g_ref[...]).astype(BF16)

    o_ref[...] = _dot(hn_ref[...], w_ref[...]).astype(o_ref.dtype)


def norm_in_proj(h, g, w):
    T = h.shape[0]
    n = w.shape[1]
    tm, tn = 1024, 1536
    return pl.pallas_call(
        _norm_matmul_kernel,
        grid=(T // tm, n // tn),
        in_specs=[pl.BlockSpec((tm, D_MODEL), lambda i, j: (i, 0)),
                  pl.BlockSpec((1, D_MODEL), lambda i, j: (0, 0)),
                  pl.BlockSpec((D_MODEL, tn), lambda i, j: (0, j))],
        out_specs=pl.BlockSpec((tm, tn), lambda i, j: (i, j)),
        out_shape=jax.ShapeDtypeStruct((T, n), BF16),
        scratch_shapes=[pltpu.VMEM((tm, D_MODEL), BF16)],
        compiler_params=_cp("parallel", "arbitrary"),
        name="norm_in_proj",
    )(h, g, w)


def relayout_w_in(w):
    d = w.shape[0]
    kpe = 3872 + 256 + 128
    return jnp.concatenate([
        w[:, :3872],
        jnp.zeros((d, P_GATE - 3872), w.dtype),
        w[:, 4288:7360],
        w[:, 3872:4256],
        jnp.tile(w[:, kpe:kpe + 16], (1, 4)),
        jnp.tile(w[:, kpe + 16:kpe + 32], (1, 4)),
    ], axis=1).astype(BF16)


def _retention_kernel(q_ref, k_ref, v_ref, g_ref, cos_ref, sin_ref, mask_ref, qd_ref, kd_ref, cd_ref,
                      o_ref, st_ref):
    @pl.when(pl.program_id(1) == 0)
    def _():
        st_ref[...] = jnp.zeros_like(st_ref)

    cos2 = cos_ref[...]
    sin2 = sin_ref[...]
    for h in range(RET_HEADS):
        sl = slice(h * RET_DK, (h + 1) * RET_DK)
        q = q_ref[:, sl].astype(F32)
        k = k_ref[:, sl].astype(F32)
        q = q * cos2 + pltpu.roll(q, 64, 1) * sin2
        k = (k * cos2 + pltpu.roll(k, 64, 1) * sin2) * (RET_DK ** -0.5)
        v = v_ref[:, sl]
        qb = q.astype(BF16)
        kb = k.astype(BF16)
        scores = _dot_nt(qb, kb) * mask_ref[h]
        o = _dot(scores.astype(BF16), v)
        st = st_ref[h]
        o = o + _dot((q * qd_ref[h]).astype(BF16), st.astype(BF16))
        st_ref[h] = st * cd_ref[h] + _dot_tn((k * kd_ref[h]).astype(BF16), v)
        mean = jnp.mean(o, axis=-1, keepdims=True)
        oc = o - mean
        var = jnp.mean(oc * oc, axis=-1, keepdims=True)
        y = oc * lax.rsqrt(var + RET_GN_EPS)
        g = g_ref[:, sl].astype(F32)
        o_ref[:, sl] = (y * (g * _sigmoid(g))).astype(o_ref.dtype)


def _retention_consts():
    hh = jnp.arange(RET_HEADS, dtype=F32)
    log_gamma = jnp.log(1.0 - 2.0 ** (-5.0 - hh))
    idx = jnp.arange(BLK, dtype=F32)
    dist = jnp.abs(idx[:, None] - idx[None, :])
    ci = np.arange(BLK) // CHUNK
    visible = jnp.asarray(ci[None, :] <= ci[:, None])
    mask = jnp.where(visible[None], jnp.exp(log_gamma[:, None, None] * dist[None]), 0.0)
    qd = jnp.exp(log_gamma[:, None] * (idx[None, :] + 1.0))
    kd = jnp.exp(log_gamma[:, None] * (BLK - 1.0 - idx[None, :]))
    cd = jnp.exp(log_gamma * BLK)
    qd = jnp.broadcast_to(qd[:, :, None], (RET_HEADS, BLK, RET_DK))
    kd = jnp.broadcast_to(kd[:, :, None], (RET_HEADS, BLK, RET_DK))
    cd = jnp.broadcast_to(cd[:, None, None], (RET_HEADS, 1, RET_DK))
    return mask, qd, kd, cd


def retention(p, tab, B, S):
    T = B * S
    nb = S // BLK
    mask, qd, kd, cd = _retention_consts()
    row = lambda c: pl.BlockSpec((BLK, 512), lambda b, i, c=c: (b * nb + i, c))
    tcol = lambda c: pl.BlockSpec((BLK, 128), lambda b, i, c=c: (b * nb + i, c))
    const3 = lambda shape: pl.BlockSpec(shape, lambda b, i: (0, 0, 0))
    return pl.pallas_call(
        _retention_kernel,
        grid=(B, nb),
        in_specs=[row(0), row(1), row(2), row(3), tcol(0), tcol(1),
                  const3((RET_HEADS, BLK, BLK)), const3((RET_HEADS, BLK, RET_DK)),
                  const3((RET_HEADS, BLK, RET_DK)), const3((RET_HEADS, 1, RET_DK))],
        out_specs=pl.BlockSpec((BLK, 512), lambda b, i: (b * nb + i, 0)),
        out_shape=jax.ShapeDtypeStruct((T, 512), BF16),
        scratch_shapes=[pltpu.VMEM((RET_HEADS, RET_DK, RET_DK), F32)],
        compiler_params=_cp("arbitrary", "arbitrary"),
        name="retention",
    )(p, p, p, p, tab, tab, mask, qd, kd, cd)


def _rwkv_pre_kernel(p_ref, mu_ref, w0_ref, w2_ref, a0_ref, a2_ref, g2_ref, kk_w_ref, ka_ref, rk_ref, seg_ref,
                     r_o, w_o, k_o, v_o, kk_o, b_o, bonus_o, g_o, prev_ref):
    tm = p_ref.shape[0]

    @pl.when(pl.program_id(1) == 0)
    def _():
        prev_ref[...] = jnp.zeros_like(prev_ref)

    x = p_ref[...].astype(F32)
    row = lax.broadcasted_iota(jnp.int32, x.shape, 0)
    shifted = jnp.where(row == 0, prev_ref[7:8, :], pltpu.roll(x, 1, 0))
    prev_ref[...] = x[tm - 8:tm, :]
    xs = x + (shifted - x) * mu_ref[...]
    r = xs[:, 0:512]
    k = xs[:, 512:1024]
    v = xs[:, 1024:1536]
    xwa = xs[:, 1536:1664]
    xg = xs[:, 1664:1920]
    z = w0_ref[...] + _dot(jnp.tanh(xwa).astype(BF16), w2_ref[...])
    logw = -_sigmoid(z) * float(np.exp(-0.5))
    a = _sigmoid(a0_ref[...] + _dot(xwa.astype(BF16), a2_ref[...]))
    g = _dot(_sigmoid(xg).astype(BF16), g2_ref[...])
    seg = seg_ref[...]
    kk = k * kk_w_ref[...]
    kk = kk * lax.rsqrt(_dot_exact_rhs(kk * kk, seg) + 1e-12)
    k = k * (1.0 + (a - 1.0) * ka_ref[...])
    bonus = _dot_exact_rhs(r * k * rk_ref[...], seg) * v
    r_o[...] = r.astype(r_o.dtype)
    w_o[...] = logw
    k_o[...] = k.astype(k_o.dtype)
    v_o[...] = v.astype(v_o.dtype)
    kk_o[...] = kk.astype(kk_o.dtype)
    b_o[...] = (kk * a).astype(b_o.dtype)
    bonus_o[...] = bonus.astype(bonus_o.dtype)
    g_o[...] = g.astype(g_o.dtype)


RWKV_PAIRS_PER_STEP = 4


def _rwkv_rec_kernel(r_ref, w_ref, k_ref, v_ref, kk_ref, b_ref, bonus_ref, g_ref, lnx_ref, o_ref, st_ref, y_ref):
    @pl.when(pl.program_id(2) == 0)
    def _():
        st_ref[...] = jnp.zeros_like(st_ref)

    row = lax.broadcasted_iota(jnp.int32, (BLK, BLK), 0)
    col = lax.broadcasted_iota(jnp.int32, (BLK, BLK), 1)
    same = (row >> 6) == (col >> 6)
    incl = same & (col <= row)
    strict = same & (col < row)
    tri = jnp.where(incl, 1.0, 0.0).astype(BF16)
    lane = lax.broadcasted_iota(jnp.int32, (BLK, 128), 1)
    head0 = lane < RWKV_HEAD
    rk_ = lax.broadcasted_iota(jnp.int32, (128, 128), 0)
    ck_ = lax.broadcasted_iota(jnp.int32, (128, 128), 1)
    bd = (rk_ >> 6) == (ck_ >> 6)
    n_chunks = BLK // CHUNK
    pairs = range(RWKV_PAIRS_PER_STEP)
    lanes = [slice(pi * 128, (pi + 1) * 128) for pi in pairs]

    vb, rt, kkt, yt, bh, kh, wc = [], [], [], [], [], [], []
    cum_all = _dot_exact_lhs(tri, w_ref[...])
    for ps in lanes:
        r = r_ref[:, ps].astype(F32)
        k = k_ref[:, ps].astype(F32)
        kk = kk_ref[:, ps].astype(F32)
        b = b_ref[:, ps].astype(F32)
        logw = w_ref[:, ps]
        cum = cum_all[:, ps]
        tot = jnp.concatenate([jnp.broadcast_to(cum[(c + 1) * CHUNK - 1:(c + 1) * CHUNK, :], (CHUNK, 128))
                               for c in range(n_chunks)], axis=0)
        e_neg = jnp.exp(-cum)
        e_end = jnp.exp(tot - cum)
        vb.append(v_ref[:, ps])
        rt.append(r * jnp.exp(cum))
        kkt.append(kk * jnp.exp(cum - logw))
        yt.append(jnp.concatenate([b * e_neg, k * e_neg], axis=0).astype(BF16))
        bh.append(b * e_end)
        kh.append(k * e_end)
        wc.append(jnp.exp(tot))

    streams = [(pi, h) for pi in pairs for h in range(2)]
    apow, xs, lrb, lrk = [], [], [], []
    for pi, h in streams:
        mh = head0 if h == 0 else jnp.logical_not(head0)
        xh = jnp.concatenate([jnp.where(mh, kkt[pi], 0.0), jnp.where(mh, rt[pi], 0.0)], axis=0).astype(BF16)
        gm = _dot_nt(xh, yt[pi])
        apow.append(jnp.where(strict, gm[:BLK, :BLK], 0.0))
        a_uk = jnp.where(strict, gm[:BLK, BLK:], 0.0)
        lrb.append(jnp.where(incl, gm[BLK:, :BLK], 0.0).astype(BF16))
        lrk.append(jnp.where(incl, gm[BLK:, BLK:], 0.0).astype(BF16))
        xs.append(jnp.concatenate([kkt[pi], _dot(a_uk.astype(BF16), vb[pi])], axis=1))
    xs = [x - _dot(a.astype(BF16), x.astype(BF16)) for a, x in zip(apow, xs)]
    for _ in range(5):
        apow = [_dot(a.astype(BF16), a.astype(BF16)) for a in apow]
        xs = [x + _dot(a.astype(BF16), x.astype(BF16)) for a, x in zip(apow, xs)]

    pm, qm, rp, y0 = [], [], [], []
    for pi in pairs:
        x0, x1 = xs[2 * pi], xs[2 * pi + 1]
        pm.append(jnp.where(head0, x0[:, :128], x1[:, :128]))
        qm.append(jnp.where(head0, x0[:, 128:], x1[:, 128:]))
    for pi in pairs:
        pq = jnp.concatenate([pm[pi], qm[pi]], axis=1).astype(BF16)
        lp0 = _dot(lrb[2 * pi], pq)
        lp1 = _dot(lrb[2 * pi + 1], pq)
        rp.append(rt[pi] - jnp.where(head0, lp0[:, :128], lp1[:, :128]))
        y0.append(jnp.where(head0, _dot(lrk[2 * pi], vb[pi]) - lp0[:, 128:],
                            _dot(lrk[2 * pi + 1], vb[pi]) - lp1[:, 128:]))

    ptb = [[None] * n_chunks for _ in pairs]
    nn = [[None] * n_chunks for _ in pairs]
    for c in range(n_chunks):
        sl = slice(c * CHUNK, (c + 1) * CHUNK)
        for pi in pairs:
            lhs = jnp.concatenate([pm[pi][sl], qm[pi][sl], vb[pi][sl].astype(F32)], axis=1).astype(BF16)
            rhs = jnp.concatenate([bh[pi][sl], kh[pi][sl]], axis=1).astype(BF16)
            zz = _dot_tn(lhs, rhs)
            ptb[pi][c] = jnp.where(bd, zz[0:128, 0:128], 0.0).astype(BF16)
            nn[pi][c] = jnp.where(bd, zz[256:384, 128:256] - zz[128:256, 0:128], 0.0)

    st = [st_ref[pi] for pi in pairs]
    for c in range(n_chunks):
        sl = slice(c * CHUNK, (c + 1) * CHUNK)
        for pi in pairs:
            stb = st[pi].astype(BF16)
            y_ref[sl, lanes[pi]] = _dot_nt(rp[pi][sl].astype(BF16), stb) + y0[pi][sl]
            st[pi] = st[pi] * wc[pi][c * CHUNK:c * CHUNK + 1, :] - _dot(stb, ptb[pi][c]) + nn[pi][c]
    for pi in pairs:
        st_ref[pi] = st[pi]

    def head_mean(x):
        s0 = jnp.sum(jnp.where(head0, x, 0.0), axis=-1, keepdims=True)
        s1 = jnp.sum(x, axis=-1, keepdims=True) - s0
        return jnp.where(head0, s0, s1) * (1.0 / RWKV_HEAD)

    for ps in lanes:
        y = y_ref[:, ps]
        yc = y - head_mean(y)
        var = head_mean(yc * yc)
        yn = yc * lax.rsqrt(var + RWKV_GN_EPS) * lnx_ref[:, ps]
        o_ref[:, ps] = ((yn + bonus_ref[:, ps].astype(F32)) * g_ref[:, ps].astype(F32)).astype(o_ref.dtype)


def rwkv7(p, prm, B, S):
    T = B * S
    tm = 256
    nb = S // tm
    pad1 = lambda a, n: jnp.pad(a.astype(F32), (0, n - a.shape[0]))[None, :]
    mu = pad1(prm['rwkv_mu'], 2048)
    w2 = jnp.zeros((128, RWKV_W), F32).at[0:64].set(prm['rwkv_w2']).astype(BF16)
    a2 = jnp.zeros((128, RWKV_W), F32).at[64:128].set(prm['rwkv_a2']).astype(BF16)
    g2 = jnp.zeros((256, RWKV_W), F32).at[0:160].set(prm['rwkv_g2']).astype(BF16)
    hid = np.arange(RWKV_W) // RWKV_HEAD
    seg = jnp.asarray(hid[:, None] == hid[None, :], BF16)
    row1 = lambda a: a.reshape(1, RWKV_W).astype(F32)
    vec = pl.BlockSpec((1, RWKV_W), lambda b, i: (0, 0))
    out_blk = pl.BlockSpec((tm, RWKV_W), lambda b, i: (b * nb + i, 0))
    sds = lambda dt: jax.ShapeDtypeStruct((T, RWKV_W), dt)
    r, logw, k, v, kk, bb, bonus, g = pl.pallas_call(
        _rwkv_pre_kernel,
        grid=(B, nb),
        in_specs=[pl.BlockSpec((tm, 2048), lambda b, i: (b * nb + i, 1)),
                  pl.BlockSpec((1, 2048), lambda b, i: (0, 0)),
                  vec, pl.BlockSpec((128, RWKV_W), lambda b, i: (0, 0)),
                  vec, pl.BlockSpec((128, RWKV_W), lambda b, i: (0, 0)),
                  pl.BlockSpec((256, RWKV_W), lambda b, i: (0, 0)),
                  vec, vec, vec,
                  pl.BlockSpec((RWKV_W, RWKV_W), lambda b, i: (0, 0))],
        out_specs=[out_blk] * 8,
        out_shape=[sds(BF16), sds(F32), sds(BF16), sds(BF16), sds(BF16), sds(BF16), sds(BF16), sds(BF16)],
        scratch_shapes=[pltpu.VMEM((8, 2048), F32)],
        compiler_params=_cp("arbitrary", "arbitrary"),
        name="rwkv_pre",
    )(p, mu, row1(prm['rwkv_w0']), w2, row1(prm['rwkv_a0']), a2, g2,
      row1(prm['rwkv_k_k']), row1(prm['rwkv_k_a']), row1(prm['rwkv_r_k']), seg)

    nblk = S // BLK
    lanes = 128 * RWKV_PAIRS_PER_STEP
    blk = pl.BlockSpec((BLK, lanes), lambda b, j, i: (b * nblk + i, j))
    return pl.pallas_call(
        _rwkv_rec_kernel,
        grid=(B, RWKV_W // lanes, nblk),
        in_specs=[blk] * 8 + [pl.BlockSpec((1, lanes), lambda b, j, i: (0, j))],
        out_specs=blk,
        out_shape=jax.ShapeDtypeStruct((T, RWKV_W), BF16),
        scratch_shapes=[pltpu.VMEM((RWKV_PAIRS_PER_STEP, 128, 128), F32), pltpu.VMEM((BLK, lanes), F32)],
        compiler_params=_cp("arbitrary", "arbitrary", "arbitrary"),
        name="rwkv_rec",
    )(r, logw, k, v, kk, bb, bonus, g, row1(prm['rwkv_lnx_w']))


MLA_TQ = 512


def _mla_prep_kernel(p_ref, tq_ref, tk_ref, qn_ref, wq_ref, kn_ref, wk_ref, wv_ref, q_o, k_o, v_o):
    x = p_ref[...].astype(F32)
    qa = x[:, 0:256]
    ckv = x[:, 256:384]
    kpe = x[:, 384:512]
    qn = (qa * lax.rsqrt(jnp.mean(qa * qa, axis=-1, keepdims=True) + NORM_EPS) * qn_ref[...]).astype(BF16)
    cn = (ckv * lax.rsqrt(jnp.mean(ckv * ckv, axis=-1, keepdims=True) + NORM_EPS) * kn_ref[...]).astype(BF16)
    q = _dot(qn, wq_ref[...])
    kn = _dot(cn, wk_ref[...])
    v = _dot(cn, wv_ref[...])
    vlane = lax.broadcasted_iota(jnp.int32, v.shape, 1) & 255
    v_o[...] = jnp.where((vlane >= 64) & (vlane < 192), 1.0, v).astype(v_o.dtype)
    prod = kpe * tk_ref[...]
    lane = lax.broadcasted_iota(jnp.int32, prod.shape, 1)
    kr = jnp.where(lane >= 64, prod + pltpu.roll(prod, 64, 1), 0.0)
    tq = tq_ref[...]
    for h in range(MLA_HEADS):
        sl = slice(h * 128, (h + 1) * 128)
        q_o[:, sl] = (q[:, sl] * tq).astype(q_o.dtype)
        k_o[:, sl] = (kn[:, sl] + kr).astype(k_o.dtype)


def _mla_flash_kernel(q_ref, k_ref, v_ref, o_ref):
    i = pl.program_id(2)
    tq = MLA_TQ
    lane = lax.broadcasted_iota(jnp.int32, (tq, 128), 1)
    rowc = lax.broadcasted_iota(jnp.int32, (tq, tq), 0) >> 6
    colc = lax.broadcasted_iota(jnp.int32, (tq, tq), 1) >> 6
    diag_mask = colc <= rowc

    rq = tq // 2
    chains = [(h, r0) for h in range(2) for r0 in (0, rq)]

    def tiles(off, carry, masked):
        scores = []
        for h, r0 in chains:
            hs = slice(h * 128, (h + 1) * 128)
            s = _dot_nt(q_ref[r0:r0 + rq, hs], k_ref[pl.ds(off, tq), hs])
            if masked:
                s = jnp.where(diag_mask[r0:r0 + rq], s, -1e30)
            scores.append(s)
        probs = []
        for s, (m, _) in zip(scores, carry):
            m_new = jnp.maximum(m, jnp.max(s, axis=-1, keepdims=True))
            probs.append((m_new, jnp.exp(m - m_new), jnp.exp((s - m_new).astype(BF16))))
        out = []
        for (h, r0), (m_new, alpha, pexp), (_, acc) in zip(chains, probs, carry):
            hs = slice(h * 128, (h + 1) * 128)
            out.append((m_new, alpha * acc + _dot(pexp, v_ref[pl.ds(off, tq), hs])))
        return tuple(out)

    def body(j, carry):
        return tiles(pl.multiple_of(j * tq, tq), carry, False)

    init = tuple((jnp.full((rq, 1), -1e30, F32), jnp.zeros((rq, 128), F32)) for _ in chains)
    carry = lax.fori_loop(0, i, body, init)
    carry = tiles(pl.multiple_of(i * tq, tq), carry, True)
    a0 = jnp.concatenate([carry[0][1], carry[1][1]], axis=0)
    a1 = jnp.concatenate([carry[2][1], carry[3][1]], axis=0)
    out = jnp.where(lane < MLA_V, a0 / pltpu.roll(a0, 64, 1), a1 / pltpu.roll(a1, 64, 1))
    o_ref[...] = out.astype(o_ref.dtype)


def mla(p, tab, q_norm, w_qb, kv_norm, w_kvb, B, S):
    T = B * S
    tm = 512
    scale = (MLA_NOPE + MLA_ROPE) ** -0.5
    wq3 = (w_qb * scale).reshape(MLA_Q_LORA, MLA_HEADS, MLA_NOPE + MLA_ROPE)
    x1 = wq3[:, :, MLA_NOPE:MLA_NOPE + 16]
    x2 = wq3[:, :, MLA_NOPE + 16:]
    wq = jnp.concatenate([wq3[:, :, :MLA_NOPE], x1, x2, x1, x2], axis=2).reshape(MLA_Q_LORA, 1024).astype(BF16)
    wkv3 = w_kvb.reshape(MLA_KV_LORA, MLA_HEADS, MLA_NOPE + MLA_V)
    wk = jnp.concatenate([wkv3[:, :, :MLA_NOPE], jnp.zeros((MLA_KV_LORA, MLA_HEADS, 64), w_kvb.dtype)],
                         axis=2).reshape(MLA_KV_LORA, 1024).astype(BF16)
    vv = wkv3[:, :, MLA_NOPE:].reshape(MLA_KV_LORA, MLA_HEADS // 2, 2, MLA_V)
    wv = jnp.concatenate([vv[:, :, 0], jnp.zeros((MLA_KV_LORA, MLA_HEADS // 2, 128), w_kvb.dtype), vv[:, :, 1]],
                         axis=2).reshape(MLA_KV_LORA, 1024).astype(BF16)
    full = lambda shape: pl.BlockSpec(shape, lambda i: (0, 0))
    q, k, v = pl.pallas_call(
        _mla_prep_kernel,
        grid=(T // tm,),
        in_specs=[pl.BlockSpec((tm, 512), lambda i: (i, P_MLA // 512)),
                  pl.BlockSpec((tm, 128), lambda i: (i, 2)),
                  pl.BlockSpec((tm, 128), lambda i: (i, 3)),
                  full((1, 256)), full((256, 1024)), full((1, 128)), full((128, 1024)), full((128, 1024))],
        out_specs=[pl.BlockSpec((tm, 1024), lambda i: (i, 0)),
                   pl.BlockSpec((tm, 1024), lambda i: (i, 0)),
                   pl.BlockSpec((tm, 1024), lambda i: (i, 0))],
        out_shape=[jax.ShapeDtypeStruct((T, 1024), BF16), jax.ShapeDtypeStruct((T, 1024), BF16),
                   jax.ShapeDtypeStruct((T, 1024), BF16)],
        compiler_params=_cp("parallel"),
        name="mla_prep",
    )(p, tab, tab, q_norm[None, :].astype(F32), wq, kv_norm[None, :].astype(F32), wk, wv)

    nq = S // MLA_TQ
    return pl.pallas_call(
        _mla_flash_kernel,
        grid=(B, MLA_HEADS // 2, nq),
        in_specs=[pl.BlockSpec((MLA_TQ, 256), lambda b, j, i: (b * nq + i, j)),
                  pl.BlockSpec((S, 256), lambda b, j, i: (b, j)),
                  pl.BlockSpec((S, 256), lambda b, j, i: (b, j))],
        out_specs=pl.BlockSpec((MLA_TQ, 128), lambda b, j, i: (b * nq + i, j)),
        out_shape=jax.ShapeDtypeStruct((T, 512), BF16),
        compiler_params=_cp("parallel", "parallel", "arbitrary"),
        name="mla_flash",
    )(q, k, v)


def _merge_kernel(yr_ref, yw_ref, ym_ref, g0_ref, g1_ref, g2_ref, h_ref, wb_ref, wo_ref, o_ref):
    merged = None
    for n, (y_ref, g_ref) in enumerate(((yr_ref, g0_ref), (yw_ref, g1_ref), (ym_ref, g2_ref))):
        term = _sigmoid(g_ref[...].astype(F32)) * _dot(y_ref[...], wb_ref[n])
        merged = term if merged is None else merged + term
    o_ref[...] = h_ref[...] + _dot(merged.astype(BF16), wo_ref[...])


def merge(y_ret, y_rwkv, y_mla, p, h, w_branch, w_o):
    T = h.shape[0]
    tm = 512
    yb = pl.BlockSpec((tm, 512), lambda i: (i, 0))
    gate = lambda n: pl.BlockSpec((tm, D_MODEL), lambda i, n=n: (i, P_GATE // D_MODEL + n))
    return pl.pallas_call(
        _merge_kernel,
        grid=(T // tm,),
        in_specs=[yb, yb, yb, gate(0), gate(1), gate(2),
                  pl.BlockSpec((tm, D_MODEL), lambda i: (i, 0)),
                  pl.BlockSpec((3, 512, D_MODEL), lambda i: (0, 0, 0)),
                  pl.BlockSpec((D_MODEL, D_MODEL), lambda i: (0, 0))],
        out_specs=pl.BlockSpec((tm, D_MODEL), lambda i: (i, 0)),
        out_shape=jax.ShapeDtypeStruct((T, D_MODEL), F32),
        compiler_params=_cp("parallel"),
        name="merge",
    )(y_ret, y_rwkv, y_mla, p, p, p, h, w_branch.astype(BF16), w_o.astype(BF16))


def _rms(x, g):
    return x * lax.rsqrt(jnp.mean(x * x, axis=-1, keepdims=True) + NORM_EPS) * g


FFN_TF = 256


def _ffn_kernel(h_ref, g_ref, wg_ref, wu_ref, wd_ref, o_ref):
    h = h_ref[...]
    hn = _rms(h, g_ref[...]).astype(BF16)
    acc = h
    for c in range(D_FF // FFN_TF):
        sl = slice(c * FFN_TF, (c + 1) * FFN_TF)
        a = _dot(hn, wg_ref[:, sl])
        u = _dot(hn, wu_ref[:, sl])
        acc = acc + _dot((a * _sigmoid(a) * u).astype(BF16), wd_ref[sl, :])
    o_ref[...] = acc


def dense_ffn(h, g, wg, wu, wd):
    T = h.shape[0]
    tm = 512
    full = lambda shape: pl.BlockSpec(shape, lambda i: (0, 0))
    return pl.pallas_call(
        _ffn_kernel,
        grid=(T // tm,),
        in_specs=[pl.BlockSpec((tm, D_MODEL), lambda i: (i, 0)), full((1, D_MODEL)),
                  full((D_MODEL, D_FF)), full((D_MODEL, D_FF)), full((D_FF, D_MODEL))],
        out_specs=pl.BlockSpec((tm, D_MODEL), lambda i: (i, 0)),
        out_shape=jax.ShapeDtypeStruct((T, D_MODEL), F32),
        compiler_params=_cp("parallel"),
        name="dense_ffn",
    )(h, g[None, :], wg.astype(BF16), wu.astype(BF16), wd.astype(BF16))


def _router_kernel(h_ref, g_ref, rhi_ref, rlo_ref, o_ref):
    hn = _rms(h_ref[...], g_ref[...])
    logits = None
    for part in _split3(hn):
        for r_ref in (rhi_ref, rlo_ref):
            t = _dot(part, r_ref[...])
            logits = t if logits is None else logits + t
    lane = lax.broadcasted_iota(jnp.int32, logits.shape, 1)
    neg = jnp.float32(-1e30)
    logits = jnp.where(lane < N_EXPERTS, logits, neg)
    m1 = jnp.max(logits, axis=-1, keepdims=True)
    i1 = jnp.min(jnp.where(logits == m1, lane, 128), axis=-1, keepdims=True)
    rest = jnp.where(lane == i1, neg, logits)
    m2 = jnp.max(rest, axis=-1, keepdims=True)
    i2 = jnp.min(jnp.where(rest == m2, lane, 128), axis=-1, keepdims=True)
    e2 = jnp.exp(m2 - m1)
    w1 = 1.0 / (1.0 + e2)
    w2 = e2 / (1.0 + e2)
    o_ref[...] = (jnp.where(lane == i1, w1, 0.0) + jnp.where(lane == i2, w2, 0.0)
                  + jnp.where(lane == 8, w1, 0.0) + jnp.where(lane == 9, w2, 0.0)
                  + jnp.where(lane == 10, i1.astype(F32), 0.0) + jnp.where(lane == 11, i2.astype(F32), 0.0))


def router(h, g, w_router):
    T = h.shape[0]
    tm = 512
    wr = jnp.pad(w_router.astype(F32), ((0, 0), (0, 128 - N_EXPERTS)))
    rhi = wr.astype(BF16)
    rlo = (wr - rhi.astype(F32)).astype(BF16)
    full = lambda shape: pl.BlockSpec(shape, lambda i: (0, 0))
    return pl.pallas_call(
        _router_kernel,
        grid=(T // tm,),
        in_specs=[pl.BlockSpec((tm, D_MODEL), lambda i: (i, 0)), full((1, D_MODEL)),
                  full((D_MODEL, 128)), full((D_MODEL, 128))],
        out_specs=pl.BlockSpec((tm, 128), lambda i: (i, 0)),
        out_shape=jax.ShapeDtypeStruct((T, 128), F32),
        compiler_params=_cp("parallel"),
        name="router",
    )(h, g[None, :], rhi, rlo)


MOE_TM = 512
MOE_TF = 1792
MOE_DISPATCH_TM = 512


def _row_copy(src_ref, src_row, dst_ref, dst_row, sem):
    return pltpu.make_async_copy(src_ref.at[pl.ds(src_row, 1)], dst_ref.at[pl.ds(dst_row, 1)], sem)


def _dispatch_kernel(pos1_ref, pos2_ref, h_ref, g_ref, xs_in_ref, xs_ref, hn_ref, sem):
    del xs_in_ref
    tm = h_ref.shape[0]
    base = pl.program_id(0) * tm
    hn_ref[...] = _rms(h_ref[...], g_ref[...])

    def issue(r, c):
        _row_copy(hn_ref, r, xs_ref, pos1_ref[base + r], sem).start()
        _row_copy(hn_ref, r, xs_ref, pos2_ref[base + r], sem).start()
        return c

    def drain(r, c):
        _row_copy(hn_ref, r, xs_ref, pos1_ref[base + r], sem).wait()
        _row_copy(hn_ref, r, xs_ref, pos2_ref[base + r], sem).wait()
        return c

    lax.fori_loop(0, tm, issue, 0, unroll=8)
    lax.fori_loop(0, tm, drain, 0, unroll=8)


def _moe_kernel(te_ref, nv_ref, xs_ref, wg_ref, wu_ref, wd_ref, y_ref, xb_ref, acc_ref):
    del te_ref
    i = pl.program_id(0)
    f = pl.program_id(1)

    @pl.when(i < nv_ref[0])
    def _():
        @pl.when(f == 0)
        def _():
            xb_ref[...] = xs_ref[...].astype(BF16)

        xb = xb_ref[...]
        a = _dot(xb, wg_ref[0])
        u = _dot(xb, wu_ref[0])
        t = _dot((a * _sigmoid(a) * u).astype(BF16), wd_ref[0])

        @pl.when(f == 0)
        def _():
            acc_ref[...] = t

        @pl.when(f > 0)
        def _():
            acc_ref[...] += t

        @pl.when(f == pl.num_programs(1) - 1)
        def _():
            y_ref[...] = acc_ref[...]

    @pl.when(i >= nv_ref[0])
    def _():
        y_ref[...] = jnp.zeros_like(y_ref)


def _combine_kernel(pos1_ref, pos2_ref, h_ref, rt_ref, fin_ref, y_ref, o_ref, ybuf_ref, sem):
    tm = h_ref.shape[0]
    base = pl.program_id(0) * tm

    def issue(r, c):
        _row_copy(y_ref, pos1_ref[base + r], ybuf_ref.at[0], r, sem).start()
        _row_copy(y_ref, pos2_ref[base + r], ybuf_ref.at[1], r, sem).start()
        return c

    def drain(r, c):
        _row_copy(y_ref, pos1_ref[base + r], ybuf_ref.at[0], r, sem).wait()
        _row_copy(y_ref, pos2_ref[base + r], ybuf_ref.at[1], r, sem).wait()
        return c

    lax.fori_loop(0, tm, issue, 0, unroll=8)
    lax.fori_loop(0, tm, drain, 0, unroll=8)
    rt = rt_ref[...]
    lane = lax.broadcasted_iota(jnp.int32, rt.shape, 1)
    w1 = jnp.sum(jnp.where(lane == 8, rt, 0.0), axis=-1, keepdims=True)
    w2 = jnp.sum(jnp.where(lane == 9, rt, 0.0), axis=-1, keepdims=True)
    o_ref[...] = _rms(h_ref[...] + (w1 * ybuf_ref[0] + w2 * ybuf_ref[1]), fin_ref[...])


def _moe_plan(rt, n_tiles):
    e1 = rt[:, 10].astype(jnp.int32)
    e2 = rt[:, 11].astype(jnp.int32)
    eid = jnp.arange(N_EXPERTS, dtype=jnp.int32)[None, :]
    m1 = (e1[:, None] == eid).astype(jnp.int32)
    m2 = (e2[:, None] == eid).astype(jnp.int32)
    m = m1 + m2
    counts = jnp.sum(m, axis=0)
    rank = jnp.cumsum(m, axis=0) - m
    padded = ((counts + MOE_TM - 1) // MOE_TM) * MOE_TM
    ends = jnp.cumsum(padded)
    pos_te = (ends - padded)[None, :] + rank
    pos1 = jnp.sum(m1 * pos_te, axis=1).astype(jnp.int32)
    pos2 = jnp.sum(m2 * pos_te, axis=1).astype(jnp.int32)
    nvalid = (ends[-1] // MOE_TM).astype(jnp.int32)
    starts = jnp.minimum(jnp.arange(n_tiles, dtype=jnp.int32), nvalid - 1) * MOE_TM
    tile_expert = jnp.sum((starts[:, None] >= ends[None, :]).astype(jnp.int32), axis=1)
    return pos1, pos2, tile_expert.astype(jnp.int32), nvalid.reshape(1)


def moe_final(h, g, rt, wg, wu, wd, final_g):
    T = h.shape[0]
    n_tiles = (2 * T) // MOE_TM + N_EXPERTS
    n_rows = n_tiles * MOE_TM
    nf = D_FF_EXPERT // MOE_TF
    pos1, pos2, tile_expert, nvalid = _moe_plan(rt, n_tiles)

    tm = MOE_DISPATCH_TM
    xs = pl.pallas_call(
        _dispatch_kernel,
        grid_spec=pltpu.PrefetchScalarGridSpec(
            num_scalar_prefetch=2,
            grid=(T // tm,),
            in_specs=[pl.BlockSpec((tm, D_MODEL), lambda i, p1, p2: (i, 0)),
                      pl.BlockSpec((1, D_MODEL), lambda i, p1, p2: (0, 0)),
                      pl.BlockSpec(memory_space=pl.ANY)],
            out_specs=pl.BlockSpec(memory_space=pl.ANY),
            scratch_shapes=[pltpu.VMEM((tm, D_MODEL), F32), pltpu.SemaphoreType.DMA(())]),
        out_shape=jax.ShapeDtypeStruct((n_rows, D_MODEL), F32),
        input_output_aliases={4: 0},
        compiler_params=_cp("arbitrary"),
        name="moe_dispatch",
    )(pos1, pos2, h, g[None, :], jnp.zeros((n_rows, D_MODEL), F32))

    def row_tile(i, f, te, nv):
        return (jnp.minimum(i, nv[0] - 1), 0)

    def w_up(i, f, te, nv):
        return (te[i], 0, jnp.where(i < nv[0], f, nf - 1))

    def w_down(i, f, te, nv):
        return (te[i], jnp.where(i < nv[0], f, nf - 1), 0)

    y = pl.pallas_call(
        _moe_kernel,
        grid_spec=pltpu.PrefetchScalarGridSpec(
            num_scalar_prefetch=2,
            grid=(n_tiles, nf),
            in_specs=[pl.BlockSpec((MOE_TM, D_MODEL), row_tile),
                      pl.BlockSpec((1, D_MODEL, MOE_TF), w_up),
                      pl.BlockSpec((1, D_MODEL, MOE_TF), w_up),
                      pl.BlockSpec((1, MOE_TF, D_MODEL), w_down)],
            out_specs=pl.BlockSpec((MOE_TM, D_MODEL), lambda i, f, te, nv: (i, 0)),
            scratch_shapes=[pltpu.VMEM((MOE_TM, D_MODEL), BF16), pltpu.VMEM((MOE_TM, D_MODEL), F32)]),
        out_shape=jax.ShapeDtypeStruct((n_rows, D_MODEL), F32),
        compiler_params=_cp("arbitrary", "arbitrary"),
        name="moe_experts",
    )(tile_expert, nvalid, xs, wg.astype(BF16), wu.astype(BF16), wd.astype(BF16))

    return pl.pallas_call(
        _combine_kernel,
        grid_spec=pltpu.PrefetchScalarGridSpec(
            num_scalar_prefetch=2,
            grid=(T // tm,),
            in_specs=[pl.BlockSpec((tm, D_MODEL), lambda i, p1, p2: (i, 0)),
                      pl.BlockSpec((tm, 128), lambda i, p1, p2: (i, 0)),
                      pl.BlockSpec((1, D_MODEL), lambda i, p1, p2: (0, 0)),
                      pl.BlockSpec(memory_space=pl.ANY)],
            out_specs=pl.BlockSpec((tm, D_MODEL), lambda i, p1, p2: (i, 0)),
            scratch_shapes=[pltpu.VMEM((2, tm, D_MODEL), F32), pltpu.SemaphoreType.DMA(())]),
        out_shape=jax.ShapeDtypeStruct((T, D_MODEL), F32),
        compiler_params=_cp("arbitrary"),
        name="moe_combine",
    )(pos1, pos2, h, rt, final_g[None, :], y)


def _final_norm_kernel(h_ref, g_ref, o_ref):
    o_ref[...] = _rms(h_ref[...], g_ref[...])


def final_norm(h, g):
    T = h.shape[0]
    tm = 1024
    return pl.pallas_call(
        _final_norm_kernel,
        grid=(T // tm,),
        in_specs=[pl.BlockSpec((tm, D_MODEL), lambda i: (i, 0)), pl.BlockSpec((1, D_MODEL), lambda i: (0, 0))],
        out_specs=pl.BlockSpec((tm, D_MODEL), lambda i: (i, 0)),
        out_shape=jax.ShapeDtypeStruct((T, D_MODEL), F32),
        compiler_params=_cp("parallel"),
        name="final_norm",
    )(h, g[None, :])


def kernel(x, positions, attn_norm, w_in, rwkv_mu, rwkv_w0, rwkv_w2, rwkv_a0, rwkv_a2, rwkv_g2, rwkv_k_k, rwkv_k_a,
           rwkv_r_k, rwkv_lnx_w, mla_q_norm, mla_w_qb, mla_kv_norm, mla_w_kvb, w_branch, w_o, ffn_norm, ffn_w_gate,
           ffn_w_up, ffn_w_down, moe_router, moe_w_gate, moe_w_up, moe_w_down, final_norm_g):
    B, S, D = x.shape
    T = B * S
    depth = attn_norm.shape[0]
    rwkv_all = dict(rwkv_mu=rwkv_mu, rwkv_w0=rwkv_w0, rwkv_w2=rwkv_w2, rwkv_a0=rwkv_a0, rwkv_a2=rwkv_a2,
                    rwkv_g2=rwkv_g2, rwkv_k_k=rwkv_k_k, rwkv_k_a=rwkv_k_a, rwkv_r_k=rwkv_r_k,
                    rwkv_lnx_w=rwkv_lnx_w)
    tab = rope_tables(positions)
    h = x.reshape(T, D)
    out = None
    for l in range(depth):
        p = norm_in_proj(h, attn_norm[l][None, :], relayout_w_in(w_in[l]))
        y_ret = retention(p, tab, B, S)
        y_rwkv = rwkv7(p, {k: v[l] for k, v in rwkv_all.items()}, B, S)
        y_mla = mla(p, tab, mla_q_norm[l], mla_w_qb[l], mla_kv_norm[l], mla_w_kvb[l], B, S)
        h = merge(y_ret, y_rwkv, y_mla, p, h, w_branch[l], w_o[l])
        last = l == depth - 1
        if l % 2 == 0:
            h = dense_ffn(h, ffn_norm[l], ffn_w_gate[l // 2], ffn_w_up[l // 2], ffn_w_down[l // 2])
            if last:
                out = final_norm(h, final_norm_g)
        else:
            rt = router(h, ffn_norm[l], moe_router[l // 2])
            out_l = moe_final(h, ffn_norm[l], rt, moe_w_gate[l // 2], moe_w_up[l // 2], moe_w_down[l // 2],
                              final_norm_g)
            if last:
                out = out_l
            else:
                raise NotImplementedError("a MoE layer that is not the last layer")
    return out.reshape(B, S, D)
```

```python
import functools

import numpy as np
import jax
import jax.numpy as jnp
from jax import lax
from jax.experimental import pallas as pl
from jax.experimental.pallas import tpu as pltpu

F32 = jnp.float32
BF16 = jnp.bfloat16

D_MODEL = 1024
CHUNK = 64
NORM_EPS = 1e-6
ROPE_BASE = 10000.0

RET_HEADS = 4
RET_DK = 128
RET_GN_EPS = 1e-5

RWKV_HEAD = 64
RWKV_HEADS = 8
RWKV_W = 512
RWKV_GN_EPS = 64e-5
RWKV_COLS = 1824

MLA_HEADS = 8
MLA_Q_LORA = 256
MLA_KV_LORA = 128
MLA_NOPE = 64
MLA_ROPE = 32
MLA_V = 64

N_EXPERTS = 8
D_FF = 2816
D_FF_EXPERT = 3584

P_RET = 0
P_RWKV = 2048
P_GATE = 4096
P_MLA = 7168
P_COLS = 7680

BLK = 256
VMEM_LIMIT_BYTES = 56 * 1024 * 1024


def _cp(*sem):
    return pltpu.CompilerParams(dimension_semantics=sem, vmem_limit_bytes=VMEM_LIMIT_BYTES)


def _sigmoid(z):
    return 1.0 / (1.0 + jnp.exp(-z))


def _split3(x):
    hi = x.astype(BF16)
    r1 = x - hi.astype(F32)
    mid = r1.astype(BF16)
    lo = (r1 - mid.astype(F32)).astype(BF16)
    return hi, mid, lo


def _dot(a, b):
    return jnp.dot(a, b, preferred_element_type=F32)


def _dot_nt(a, b):
    return lax.dot_general(a, b, (((1,), (1,)), ((), ())), preferred_element_type=F32)


def _dot_tn(a, b):
    return lax.dot_general(a, b, (((0,), (0,)), ((), ())), preferred_element_type=F32)


def _dot_exact_lhs(m_bf16, x_f32):
    hi, mid, lo = _split3(x_f32)
    return _dot(m_bf16, hi) + _dot(m_bf16, mid) + _dot(m_bf16, lo)


def _dot_exact_rhs(x_f32, m_bf16):
    hi, mid, lo = _split3(x_f32)
    return _dot(hi, m_bf16) + _dot(mid, m_bf16) + _dot(lo, m_bf16)


def _dot_split2_rhs(x_f32, m_bf16):
    hi, mid, _ = _split3(x_f32)
    return _dot(hi, m_bf16) + _dot(mid, m_bf16)


TAB_COLS = 768


def _tables_kernel(pos_ref, freq_ref, perm_ref, o_ref):
    ang = pos_ref[...] * freq_ref[...]
    cs = jnp.concatenate([jnp.cos(ang), jnp.sin(ang)], axis=1)
    o_ref[...] = _dot_exact_rhs(cs, perm_ref[...])


def _table_perm():
    p = np.zeros((256, TAB_COLS), np.float32)
    c, s = 0, 128
    for j in range(64):
        p[c + j, j] = 1.0
        p[c + j, 64 + j] = 1.0
        p[s + j, 128 + j] = -1.0
        p[s + j, 128 + 64 + j] = 1.0
    for j in range(64):
        p[c + 127, 256 + j] = 1.0
    for j in range(16):
        cj, sj = c + 64 + j, s + 64 + j
        q0 = 256 + 64
        p[cj, q0 + j] = 1.0
        p[sj, q0 + 16 + j] = 1.0
        p[sj, q0 + 32 + j] = 1.0
        p[cj, q0 + 48 + j] = 1.0
        k0 = 384
        p[cj, k0 + j] = 1.0
        p[cj, k0 + 16 + j] = -1.0
        p[sj, k0 + 32 + j] = 1.0
        p[sj, k0 + 48 + j] = 1.0
        p[sj, k0 + 64 + j] = -1.0
        p[sj, k0 + 80 + j] = 1.0
        p[cj, k0 + 96 + j] = 1.0
        p[cj, k0 + 112 + j] = 1.0
    return p


def rope_tables(positions):
    T = positions.size
    tm = 512
    inv_ret = 1.0 / (ROPE_BASE ** (jnp.arange(0, RET_DK, 2, dtype=F32) / RET_DK))
    inv_mla = 1.0 / (ROPE_BASE ** (jnp.arange(0, MLA_ROPE, 2, dtype=F32) / MLA_ROPE))
    freq = jnp.concatenate([inv_ret, inv_mla, jnp.zeros((48,), F32)])[None, :]
    pos = positions.reshape(T, 1).astype(F32)
    perm = jnp.asarray(_table_perm(), BF16)
    return pl.pallas_call(
        _tables_kernel,
        grid=(T // tm,),
        in_specs=[pl.BlockSpec((tm, 1), lambda i: (i, 0)),
                  pl.BlockSpec((1, 128), lambda i: (0, 0)),
                  pl.BlockSpec((256, TAB_COLS), lambda i: (0, 0))],
        out_specs=pl.BlockSpec((tm, TAB_COLS), lambda i: (i, 0)),
        out_shape=jax.ShapeDtypeStruct((T, TAB_COLS), F32),
        compiler_params=_cp("parallel"),
        name="rope_tables",
    )(pos, freq, perm)


def _norm_matmul_kernel(x_ref, g_ref, w_ref, o_ref, hn_ref):
    @pl.when(pl.program_id(1) == 0)
    def _():
        x = x_ref[...]
        ms = jnp.mean(x * x, axis=-1, keepdims=True)
        hn_ref[...] = (x * lax.rsqrt(ms + NORM_EPS) * g_ref[...]).astype(BF16)

    o_ref[...] = _dot(hn_ref[...], w_ref[...]).astype(o_ref.dtype)


def norm_in_proj(h, g, w):
    T = h.shape[0]
    n = w.shape[1]
    tm, tn = 1024, 1536
    return pl.pallas_call(
        _norm_matmul_kernel,
        grid=(T // tm, n // tn),
        in_specs=[pl.BlockSpec((tm, D_MODEL), lambda i, j: (i, 0)),
                  pl.BlockSpec((1, D_MODEL), lambda i, j: (0, 0)),
                  pl.BlockSpec((D_MODEL, tn), lambda i, j: (0, j))],
        out_specs=pl.BlockSpec((tm, tn), lambda i, j: (i, j)),
        out_shape=jax.ShapeDtypeStruct((T, n), BF16),
        scratch_shapes=[pltpu.VMEM((tm, D_MODEL), BF16)],
        compiler_params=_cp("parallel", "arbitrary"),
        name="norm_in_proj",
    )(h, g, w)


def relayout_w_in(w):
    d = w.shape[0]
    kpe = 3872 + 256 + 128
    return jnp.concatenate([
        w[:, :3872],
        jnp.zeros((d, P_GATE - 3872), w.dtype),
        w[:, 4288:7360],
        w[:, 3872:4256],
        jnp.tile(w[:, kpe:kpe + 16], (1, 4)),
        jnp.tile(w[:, kpe + 16:kpe + 32], (1, 4)),
    ], axis=1).astype(BF16)


def _retention_kernel(q_ref, k_ref, v_ref, g_ref, cos_ref, sin_ref, mask_ref, qd_ref, kd_ref, cd_ref,
                      o_ref, st_ref):
    @pl.when(pl.program_id(1) == 0)
    def _():
        st_ref[...] = jnp.zeros_like(st_ref)

    cos2 = cos_ref[...]
    sin2 = sin_ref[...]
    for h in range(RET_HEADS):
        sl = slice(h * RET_DK, (h + 1) * RET_DK)
        q = q_ref[:, sl].astype(F32)
        k = k_ref[:, sl].astype(F32)
        q = q * cos2 + pltpu.roll(q, 64, 1) * sin2
        k = (k * cos2 + pltpu.roll(k, 64, 1) * sin2) * (RET_DK ** -0.5)
        v = v_ref[:, sl]
        qb = q.astype(BF16)
        kb = k.astype(BF16)
        scores = _dot_nt(qb, kb) * mask_ref[h]
        o = _dot(scores.astype(BF16), v)
        st = st_ref[h]
        o = o + _dot((q * qd_ref[h]).astype(BF16), st.astype(BF16))
        st_ref[h] = st * cd_ref[h] + _dot_tn((k * kd_ref[h]).astype(BF16), v)
        mean = jnp.mean(o, axis=-1, keepdims=True)
        oc = o - mean
        var = jnp.mean(oc * oc, axis=-1, keepdims=True)
        y = oc * lax.rsqrt(var + RET_GN_EPS)
        g = g_ref[:, sl].astype(F32)
        o_ref[:, sl] = (y * (g * _sigmoid(g))).astype(o_ref.dtype)


def _retention_consts():
    hh = jnp.arange(RET_HEADS, dtype=F32)
    log_gamma = jnp.log(1.0 - 2.0 ** (-5.0 - hh))
    idx = jnp.arange(BLK, dtype=F32)
    dist = jnp.abs(idx[:, None] - idx[None, :])
    ci = np.arange(BLK) // CHUNK
    visible = jnp.asarray(ci[None, :] <= ci[:, None])
    mask = jnp.where(visible[None], jnp.exp(log_gamma[:, None, None] * dist[None]), 0.0)
    qd = jnp.exp(log_gamma[:, None] * (idx[None, :] + 1.0))
    kd = jnp.exp(log_gamma[:, None] * (BLK - 1.0 - idx[None, :]))
    cd = jnp.exp(log_gamma * BLK)
    qd = jnp.broadcast_to(qd[:, :, None], (RET_HEADS, BLK, RET_DK))
    kd = jnp.broadcast_to(kd[:, :, None], (RET_HEADS, BLK, RET_DK))
    cd = jnp.broadcast_to(cd[:, None, None], (RET_HEADS, 1, RET_DK))
    return mask, qd, kd, cd


def retention(p, tab, B, S):
    T = B * S
    nb = S // BLK
    mask, qd, kd, cd = _retention_consts()
    row = lambda c: pl.BlockSpec((BLK, 512), lambda b, i, c=c: (b * nb + i, c))
    tcol = lambda c: pl.BlockSpec((BLK, 128), lambda b, i, c=c: (b * nb + i, c))
    const3 = lambda shape: pl.BlockSpec(shape, lambda b, i: (0, 0, 0))
    return pl.pallas_call(
        _retention_kernel,
        grid=(B, nb),
        in_specs=[row(0), row(1), row(2), row(3), tcol(0), tcol(1),
                  const3((RET_HEADS, BLK, BLK)), const3((RET_HEADS, BLK, RET_DK)),
                  const3((RET_HEADS, BLK, RET_DK)), const3((RET_HEADS, 1, RET_DK))],
        out_specs=pl.BlockSpec((BLK, 512), lambda b, i: (b * nb + i, 0)),
        out_shape=jax.ShapeDtypeStruct((T, 512), BF16),
        scratch_shapes=[pltpu.VMEM((RET_HEADS, RET_DK, RET_DK), F32)],
        compiler_params=_cp("arbitrary", "arbitrary"),
        name="retention",
    )(p, p, p, p, tab, tab, mask, qd, kd, cd)


def _rwkv_pre_kernel(p_ref, mu_ref, w0_ref, w2_ref, a0_ref, a2_ref, g2_ref, kk_w_ref, ka_ref, rk_ref, seg_ref,
                     r_o, w_o, k_o, v_o, kk_o, b_o, bonus_o, g_o, prev_ref):
    tm = p_ref.shape[0]

    @pl.when(pl.program_id(1) == 0)
    def _():
        prev_ref[...] = jnp.zeros_like(prev_ref)

    x = p_ref[...].astype(F32)
    row = lax.broadcasted_iota(jnp.int32, x.shape, 0)
    shifted = jnp.where(row == 0, prev_ref[7:8, :], pltpu.roll(x, 1, 0))
    prev_ref[...] = x[tm - 8:tm, :]
    xs = x + (shifted - x) * mu_ref[...]
    r = xs[:, 0:512]
    k = xs[:, 512:1024]
    v = xs[:, 1024:1536]
    xwa = xs[:, 1536:1664]
    xg = xs[:, 1664:1920]
    z = w0_ref[...] + _dot(jnp.tanh(xwa).astype(BF16), w2_ref[...])
    logw = -_sigmoid(z) * float(np.exp(-0.5))
    a = _sigmoid(a0_ref[...] + _dot(xwa.astype(BF16), a2_ref[...]))
    g = _dot(_sigmoid(xg).astype(BF16), g2_ref[...])
    seg = seg_ref[...]
    kk = k * kk_w_ref[...]
    kk = kk * lax.rsqrt(_dot_split2_rhs(kk * kk, seg) + 1e-12)
    k = k * (1.0 + (a - 1.0) * ka_ref[...])
    bonus = _dot_split2_rhs(r * k * rk_ref[...], seg) * v
    r_o[...] = r.astype(r_o.dtype)
    w_o[...] = logw
    k_o[...] = k.astype(k_o.dtype)
    v_o[...] = v.astype(v_o.dtype)
    kk_o[...] = kk.astype(kk_o.dtype)
    b_o[...] = (kk * a).astype(b_o.dtype)
    bonus_o[...] = bonus.astype(bonus_o.dtype)
    g_o[...] = g.astype(g_o.dtype)


RWKV_PAIRS_PER_STEP = 4


def _rwkv_rec_kernel(r_ref, w_ref, k_ref, v_ref, kk_ref, b_ref, bonus_ref, g_ref, lnx_ref, o_ref, st_ref, y_ref):
    @pl.when(pl.program_id(2) == 0)
    def _():
        st_ref[...] = jnp.zeros_like(st_ref)

    row = lax.broadcasted_iota(jnp.int32, (BLK, BLK), 0)
    col = lax.broadcasted_iota(jnp.int32, (BLK, BLK), 1)
    same = (row >> 6) == (col >> 6)
    incl = same & (col <= row)
    strict = same & (col < row)
    tri = jnp.where(incl, 1.0, 0.0).astype(BF16)
    lane = lax.broadcasted_iota(jnp.int32, (BLK, 128), 1)
    head0 = lane < RWKV_HEAD
    rk_ = lax.broadcasted_iota(jnp.int32, (128, 128), 0)
    ck_ = lax.broadcasted_iota(jnp.int32, (128, 128), 1)
    bd = (rk_ >> 6) == (ck_ >> 6)
    n_chunks = BLK // CHUNK
    pairs = range(RWKV_PAIRS_PER_STEP)
    lanes = [slice(pi * 128, (pi + 1) * 128) for pi in pairs]

    vb, rt, kkt, yt, bh, kh, wc = [], [], [], [], [], [], []
    cum_all = _dot_exact_lhs(tri, w_ref[...])
    for ps in lanes:
        r = r_ref[:, ps].astype(F32)
        k = k_ref[:, ps].astype(F32)
        kk = kk_ref[:, ps].astype(F32)
        b = b_ref[:, ps].astype(F32)
        logw = w_ref[:, ps]
        cum = cum_all[:, ps]
        tot = jnp.concatenate([jnp.broadcast_to(cum[(c + 1) * CHUNK - 1:(c + 1) * CHUNK, :], (CHUNK, 128))
                               for c in range(n_chunks)], axis=0)
        e_neg = jnp.exp(-cum)
        e_end = jnp.exp(tot - cum)
        vb.append(v_ref[:, ps])
        rt.append(r * jnp.exp(cum))
        kkt.append(kk * jnp.exp(cum - logw))
        yt.append(jnp.concatenate([b * e_neg, k * e_neg], axis=0).astype(BF16))
        bh.append(b * e_end)
        kh.append(k * e_end)
        wc.append(jnp.exp(tot))

    streams = [(pi, h) for pi in pairs for h in range(2)]
    apow, xs, lrb, lrk = [], [], [], []
    for pi, h in streams:
        mh = head0 if h == 0 else jnp.logical_not(head0)
        xh = jnp.concatenate([jnp.where(mh, kkt[pi], 0.0), jnp.where(mh, rt[pi], 0.0)], axis=0).astype(BF16)
        gm = _dot_nt(xh, yt[pi])
        apow.append(jnp.where(strict, gm[:BLK, :BLK], 0.0))
        a_uk = jnp.where(strict, gm[:BLK, BLK:], 0.0)
        lrb.append(jnp.where(incl, gm[BLK:, :BLK], 0.0).astype(BF16))
        lrk.append(jnp.where(incl, gm[BLK:, BLK:], 0.0).astype(BF16))
        xs.append(jnp.concatenate([kkt[pi], _dot(a_uk.astype(BF16), vb[pi])], axis=1))
    xs = [x - _dot(a.astype(BF16), x.astype(BF16)) for a, x in zip(apow, xs)]
    for _ in range(5):
        apow = [_dot(a.astype(BF16), a.astype(BF16)) for a in apow]
        xs = [x + _dot(a.astype(BF16), x.astype(BF16)) for a, x in zip(apow, xs)]

    pm, qm, rp, y0 = [], [], [], []
    for pi in pairs:
        x0, x1 = xs[2 * pi], xs[2 * pi + 1]
        pm.append(jnp.where(head0, x0[:, :128], x1[:, :128]))
        qm.append(jnp.where(head0, x0[:, 128:], x1[:, 128:]))
    for pi in pairs:
        pq = jnp.concatenate([pm[pi], qm[pi]], axis=1).astype(BF16)
        lp0 = _dot(lrb[2 * pi], pq)
        lp1 = _dot(lrb[2 * pi + 1], pq)
        rp.append(rt[pi] - jnp.where(head0, lp0[:, :128], lp1[:, :128]))
        y0.append(jnp.where(head0, _dot(lrk[2 * pi], vb[pi]) - lp0[:, 128:],
                            _dot(lrk[2 * pi + 1], vb[pi]) - lp1[:, 128:]))

    ptb = [[None] * n_chunks for _ in pairs]
    nn = [[None] * n_chunks for _ in pairs]
    for c in range(n_chunks):
        sl = slice(c * CHUNK, (c + 1) * CHUNK)
        for pi in pairs:
            lhs = jnp.concatenate([pm[pi][sl], qm[pi][sl], vb[pi][sl].astype(F32)], axis=1).astype(BF16)
            rhs = jnp.concatenate([bh[pi][sl], kh[pi][sl]], axis=1).astype(BF16)
            zz = _dot_tn(lhs, rhs)
            ptb[pi][c] = jnp.where(bd, zz[0:128, 0:128], 0.0).astype(BF16)
            nn[pi][c] = jnp.where(bd, zz[256:384, 128:256] - zz[128:256, 0:128], 0.0)

    st = [st_ref[pi] for pi in pairs]
    for c in range(n_chunks):
        sl = slice(c * CHUNK, (c + 1) * CHUNK)
        for pi in pairs:
            stb = st[pi].astype(BF16)
            y_ref[sl, lanes[pi]] = _dot_nt(rp[pi][sl].astype(BF16), stb) + y0[pi][sl]
            st[pi] = st[pi] * wc[pi][c * CHUNK:c * CHUNK + 1, :] - _dot(stb, ptb[pi][c]) + nn[pi][c]
    for pi in pairs:
        st_ref[pi] = st[pi]

    def head_mean(x):
        s0 = jnp.sum(jnp.where(head0, x, 0.0), axis=-1, keepdims=True)
        s1 = jnp.sum(x, axis=-1, keepdims=True) - s0
        return jnp.where(head0, s0, s1) * (1.0 / RWKV_HEAD)

    for ps in lanes:
        y = y_ref[:, ps]
        yc = y - head_mean(y)
        var = head_mean(yc * yc)
        yn = yc * lax.rsqrt(var + RWKV_GN_EPS) * lnx_ref[:, ps]
        o_ref[:, ps] = ((yn + bonus_ref[:, ps].astype(F32)) * g_ref[:, ps].astype(F32)).astype(o_ref.dtype)


def rwkv7(p, prm, B, S):
    T = B * S
    tm = 256
    nb = S // tm
    pad1 = lambda a, n: jnp.pad(a.astype(F32), (0, n - a.shape[0]))[None, :]
    mu = pad1(prm['rwkv_mu'], 2048)
    w2 = jnp.zeros((128, RWKV_W), F32).at[0:64].set(prm['rwkv_w2']).astype(BF16)
    a2 = jnp.zeros((128, RWKV_W), F32).at[64:128].set(prm['rwkv_a2']).astype(BF16)
    g2 = jnp.zeros((256, RWKV_W), F32).at[0:160].set(prm['rwkv_g2']).astype(BF16)
    hid = np.arange(RWKV_W) // RWKV_HEAD
    seg = jnp.asarray(hid[:, None] == hid[None, :], BF16)
    row1 = lambda a: a.reshape(1, RWKV_W).astype(F32)
    vec = pl.BlockSpec((1, RWKV_W), lambda b, i: (0, 0))
    out_blk = pl.BlockSpec((tm, RWKV_W), lambda b, i: (b * nb + i, 0))
    sds = lambda dt: jax.ShapeDtypeStruct((T, RWKV_W), dt)
    r, logw, k, v, kk, bb, bonus, g = pl.pallas_call(
        _rwkv_pre_kernel,
        grid=(B, nb),
        in_specs=[pl.BlockSpec((tm, 2048), lambda b, i: (b * nb + i, 1)),
                  pl.BlockSpec((1, 2048), lambda b, i: (0, 0)),
                  vec, pl.BlockSpec((128, RWKV_W), lambda b, i: (0, 0)),
                  vec, pl.BlockSpec((128, RWKV_W), lambda b, i: (0, 0)),
                  pl.BlockSpec((256, RWKV_W), lambda b, i: (0, 0)),
                  vec, vec, vec,
                  pl.BlockSpec((RWKV_W, RWKV_W), lambda b, i: (0, 0))],
        out_specs=[out_blk] * 8,
        out_shape=[sds(BF16), sds(F32), sds(BF16), sds(BF16), sds(BF16), sds(BF16), sds(BF16), sds(BF16)],
        scratch_shapes=[pltpu.VMEM((8, 2048), F32)],
        compiler_params=_cp("arbitrary", "arbitrary"),
        name="rwkv_pre",
    )(p, mu, row1(prm['rwkv_w0']), w2, row1(prm['rwkv_a0']), a2, g2,
      row1(prm['rwkv_k_k']), row1(prm['rwkv_k_a']), row1(prm['rwkv_r_k']), seg)

    nblk = S // BLK
    lanes = 128 * RWKV_PAIRS_PER_STEP
    blk = pl.BlockSpec((BLK, lanes), lambda b, j, i: (b * nblk + i, j))
    return pl.pallas_call(
        _rwkv_rec_kernel,
        grid=(B, RWKV_W // lanes, nblk),
        in_specs=[blk] * 8 + [pl.BlockSpec((1, lanes), lambda b, j, i: (0, j))],
        out_specs=blk,
        out_shape=jax.ShapeDtypeStruct((T, RWKV_W), BF16),
        scratch_shapes=[pltpu.VMEM((RWKV_PAIRS_PER_STEP, 128, 128), F32), pltpu.VMEM((BLK, lanes), F32)],
        compiler_params=_cp("arbitrary", "arbitrary", "arbitrary"),
        name="rwkv_rec",
    )(r, logw, k, v, kk, bb, bonus, g, row1(prm['rwkv_lnx_w']))


MLA_TQ = 1024
MLA_TK = 512
MLA_RQ = 256
MLA_VT = 80


def _mla_prep_kernel(p_ref, tq_ref, tk_ref, qn_ref, wq_ref, kn_ref, wk_ref, wv_ref, q_o, k_o, v_o):
    x = p_ref[...].astype(F32)
    qa = x[:, 0:256]
    ckv = x[:, 256:384]
    kpe = x[:, 384:512]
    qn = (qa * lax.rsqrt(jnp.mean(qa * qa, axis=-1, keepdims=True) + NORM_EPS) * qn_ref[...]).astype(BF16)
    cn = (ckv * lax.rsqrt(jnp.mean(ckv * ckv, axis=-1, keepdims=True) + NORM_EPS) * kn_ref[...]).astype(BF16)
    q = _dot(qn, wq_ref[...])
    kn = _dot(cn, wk_ref[...])
    vt = _dot_nt(wv_ref[...], cn)
    vrow = lax.rem(lax.broadcasted_iota(jnp.int32, vt.shape, 0), MLA_VT)
    v_o[...] = jnp.where(vrow >= MLA_V, 1.0, vt).astype(v_o.dtype)
    prod = kpe * tk_ref[...]
    lane = lax.broadcasted_iota(jnp.int32, prod.shape, 1)
    kr = jnp.where(lane >= 64, prod + pltpu.roll(prod, 64, 1), 0.0)
    tq = tq_ref[...]
    for h in range(MLA_HEADS):
        sl = slice(h * 128, (h + 1) * 128)
        q_o[:, sl] = (q[:, sl] * tq).astype(q_o.dtype)
        k_o[:, sl] = (kn[:, sl] + kr).astype(k_o.dtype)


def _mla_flash_kernel(q_ref, k_ref, v_ref, o_ref):
    i = pl.program_id(2)
    tq, tk, rq = MLA_TQ, MLA_TK, MLA_RQ
    keyc = lax.broadcasted_iota(jnp.int32, (tk, tk), 0) >> 6
    qryc = lax.broadcasted_iota(jnp.int32, (tk, tk), 1) >> 6
    diag_mask = keyc <= qryc

    chains = [(h, r0) for h in range(2) for r0 in range(0, tq, rq)]

    def tiles(off, carry, first_diag_row):
        def mask_of(r0):
            if first_diag_row is None or r0 >= first_diag_row + tk:
                return "all"
            if r0 < first_diag_row:
                return "none"
            return diag_mask[:, r0 - first_diag_row:r0 - first_diag_row + rq]

        active = [c for c, (_, r0) in enumerate(chains) if not isinstance(mask_of(r0), str) or mask_of(r0) == "all"]
        scores = {}
        for c in active:
            h, r0 = chains[c]
            hs = slice(h * 128, (h + 1) * 128)
            s = _dot_nt(k_ref[pl.ds(off, tk), hs], q_ref[r0:r0 + rq, hs])
            mk = mask_of(r0)
            if not isinstance(mk, str):
                s = jnp.where(mk, s, -1e30)
            scores[c] = s.astype(BF16)
        probs = {}
        for c in active:
            m = carry[c][0]
            m_new = jnp.maximum(m, jnp.max(scores[c], axis=0, keepdims=True).astype(F32))
            probs[c] = (m_new, jnp.exp(m - m_new), jnp.exp(scores[c] - m_new.astype(BF16)))
        out = list(carry)
        for c in active:
            h, r0 = chains[c]
            m_new, alpha, pexp = probs[c]
            vt = v_ref[h * MLA_VT:(h + 1) * MLA_VT, pl.ds(off, tk)]
            out[c] = (m_new, alpha * carry[c][1] + _dot(vt, pexp))
        return tuple(out)

    def body(j, carry):
        for d in range(tq // tk):
            carry = tiles(pl.multiple_of(j * tq + d * tk, tk), carry, None)
        return carry

    init = tuple((jnp.full((1, rq), -1e30, F32), jnp.zeros((MLA_VT, rq), F32)) for _ in chains)
    carry = lax.fori_loop(0, i, body, init)
    for d in range(tq // tk):
        carry = tiles(pl.multiple_of(i * tq + d * tk, tk), carry, d * tk)
    per_head = len(chains) // 2
    heads = []
    for h in range(2):
        acc = jnp.concatenate([c[1] for c in carry[h * per_head:(h + 1) * per_head]], axis=1)
        heads.append(acc[:MLA_V] / acc[MLA_V:MLA_V + 1])
    o_ref[...] = jnp.concatenate(heads, axis=0).T.astype(o_ref.dtype)


def mla(p, tab, q_norm, w_qb, kv_norm, w_kvb, B, S):
    T = B * S
    tm = 512
    scale = (MLA_NOPE + MLA_ROPE) ** -0.5
    wq3 = (w_qb * scale).reshape(MLA_Q_LORA, MLA_HEADS, MLA_NOPE + MLA_ROPE)
    x1 = wq3[:, :, MLA_NOPE:MLA_NOPE + 16]
    x2 = wq3[:, :, MLA_NOPE + 16:]
    wq = jnp.concatenate([wq3[:, :, :MLA_NOPE], x1, x2, x1, x2], axis=2).reshape(MLA_Q_LORA, 1024).astype(BF16)
    wkv3 = w_kvb.reshape(MLA_KV_LORA, MLA_HEADS, MLA_NOPE + MLA_V)
    wk = jnp.concatenate([wkv3[:, :, :MLA_NOPE], jnp.zeros((MLA_KV_LORA, MLA_HEADS, 64), w_kvb.dtype)],
                         axis=2).reshape(MLA_KV_LORA, 1024).astype(BF16)
    wv = jnp.concatenate([wkv3[:, :, MLA_NOPE:], jnp.zeros((MLA_KV_LORA, MLA_HEADS, MLA_VT - MLA_V), w_kvb.dtype)],
                         axis=2).reshape(MLA_KV_LORA, MLA_HEADS * MLA_VT).T.astype(BF16)
    full = lambda shape: pl.BlockSpec(shape, lambda i: (0, 0))
    q, k, v = pl.pallas_call(
        _mla_prep_kernel,
        grid=(T // tm,),
        in_specs=[pl.BlockSpec((tm, 512), lambda i: (i, P_MLA // 512)),
                  pl.BlockSpec((tm, 128), lambda i: (i, 2)),
                  pl.BlockSpec((tm, 128), lambda i: (i, 3)),
                  full((1, 256)), full((256, 1024)), full((1, 128)), full((128, 1024)),
                  full((MLA_HEADS * MLA_VT, 128))],
        out_specs=[pl.BlockSpec((tm, 1024), lambda i: (i, 0)),
                   pl.BlockSpec((tm, 1024), lambda i: (i, 0)),
                   pl.BlockSpec((MLA_HEADS * MLA_VT, tm), lambda i: (0, i))],
        out_shape=[jax.ShapeDtypeStruct((T, 1024), BF16), jax.ShapeDtypeStruct((T, 1024), BF16),
                   jax.ShapeDtypeStruct((MLA_HEADS * MLA_VT, T), BF16)],
        compiler_params=_cp("parallel"),
        name="mla_prep",
    )(p, tab, tab, q_norm[None, :].astype(F32), wq, kv_norm[None, :].astype(F32), wk, wv)

    nq = S // MLA_TQ
    return pl.pallas_call(
        _mla_flash_kernel,
        grid=(B, MLA_HEADS // 2, nq),
        in_specs=[pl.BlockSpec((MLA_TQ, 256), lambda b, j, i: (b * nq + i, j)),
                  pl.BlockSpec((S, 256), lambda b, j, i: (b, j)),
                  pl.BlockSpec((2 * MLA_VT, S), lambda b, j, i: (j, b))],
        out_specs=pl.BlockSpec((MLA_TQ, 128), lambda b, j, i: (b * nq + i, j)),
        out_shape=jax.ShapeDtypeStruct((T, 512), BF16),
        compiler_params=_cp("parallel", "parallel", "arbitrary"),
        name="mla_flash",
    )(q, k, v)


def _merge_kernel(yr_ref, yw_ref, ym_ref, g0_ref, g1_ref, g2_ref, h_ref, wb_ref, wo_ref, o_ref):
    merged = None
    for n, (y_ref, g_ref) in enumerate(((yr_ref, g0_ref), (yw_ref, g1_ref), (ym_ref, g2_ref))):
        term = _sigmoid(g_ref[...].astype(F32)) * _dot(y_ref[...], wb_ref[n])
        merged = term if merged is None else merged + term
    o_ref[...] = h_ref[...] + _dot(merged.astype(BF16), wo_ref[...])


def merge(y_ret, y_rwkv, y_mla, p, h, w_branch, w_o):
    T = h.shape[0]
    tm = 512
    yb = pl.BlockSpec((tm, 512), lambda i: (i, 0))
    gate = lambda n: pl.BlockSpec((tm, D_MODEL), lambda i, n=n: (i, P_GATE // D_MODEL + n))
    return pl.pallas_call(
        _merge_kernel,
        grid=(T // tm,),
        in_specs=[yb, yb, yb, gate(0), gate(1), gate(2),
                  pl.BlockSpec((tm, D_MODEL), lambda i: (i, 0)),
                  pl.BlockSpec((3, 512, D_MODEL), lambda i: (0, 0, 0)),
                  pl.BlockSpec((D_MODEL, D_MODEL), lambda i: (0, 0))],
        out_specs=pl.BlockSpec((tm, D_MODEL), lambda i: (i, 0)),
        out_shape=jax.ShapeDtypeStruct((T, D_MODEL), F32),
        compiler_params=_cp("parallel"),
        name="merge",
    )(y_ret, y_rwkv, y_mla, p, p, p, h, w_branch.astype(BF16), w_o.astype(BF16))


def _rms(x, g):
    return x * lax.rsqrt(jnp.mean(x * x, axis=-1, keepdims=True) + NORM_EPS) * g


FFN_TF = 256


def _ffn_kernel(h_ref, g_ref, wg_ref, wu_ref, wd_ref, o_ref):
    h = h_ref[...]
    hn = _rms(h, g_ref[...]).astype(BF16)
    acc = h
    for c in range(D_FF // FFN_TF):
        sl = slice(c * FFN_TF, (c + 1) * FFN_TF)
        a = _dot(hn, wg_ref[:, sl])
        u = _dot(hn, wu_ref[:, sl])
        acc = acc + _dot((a * _sigmoid(a) * u).astype(BF16), wd_ref[sl, :])
    o_ref[...] = acc


def dense_ffn(h, g, wg, wu, wd):
    T = h.shape[0]
    tm = 512
    full = lambda shape: pl.BlockSpec(shape, lambda i: (0, 0))
    return pl.pallas_call(
        _ffn_kernel,
        grid=(T // tm,),
        in_specs=[pl.BlockSpec((tm, D_MODEL), lambda i: (i, 0)), full((1, D_MODEL)),
                  full((D_MODEL, D_FF)), full((D_MODEL, D_FF)), full((D_FF, D_MODEL))],
        out_specs=pl.BlockSpec((tm, D_MODEL), lambda i: (i, 0)),
        out_shape=jax.ShapeDtypeStruct((T, D_MODEL), F32),
        compiler_params=_cp("parallel"),
        name="dense_ffn",
    )(h, g[None, :], wg.astype(BF16), wu.astype(BF16), wd.astype(BF16))


def _router_kernel(h_ref, g_ref, rhi_ref, rlo_ref, o_ref):
    hn = _rms(h_ref[...], g_ref[...])
    hi, mid, _ = _split3(hn)
    logits = _dot(hi, rhi_ref[...]) + (_dot(hi, rlo_ref[...]) + _dot(mid, rhi_ref[...]))
    lane = lax.broadcasted_iota(jnp.int32, logits.shape, 1)
    neg = jnp.float32(-1e30)
    logits = jnp.where(lane < N_EXPERTS, logits, neg)
    m1 = jnp.max(logits, axis=-1, keepdims=True)
    i1 = jnp.min(jnp.where(logits == m1, lane, 128), axis=-1, keepdims=True)
    rest = jnp.where(lane == i1, neg, logits)
    m2 = jnp.max(rest, axis=-1, keepdims=True)
    i2 = jnp.min(jnp.where(rest == m2, lane, 128), axis=-1, keepdims=True)
    e2 = jnp.exp(m2 - m1)
    w1 = 1.0 / (1.0 + e2)
    w2 = e2 / (1.0 + e2)
    o_ref[...] = (jnp.where(lane == i1, w1, 0.0) + jnp.where(lane == i2, w2, 0.0)
                  + jnp.where(lane == 8, w1, 0.0) + jnp.where(lane == 9, w2, 0.0)
                  + jnp.where(lane == 10, i1.astype(F32), 0.0) + jnp.where(lane == 11, i2.astype(F32), 0.0))


def router(h, g, w_router):
    T = h.shape[0]
    tm = 512
    wr = jnp.pad(w_router.astype(F32), ((0, 0), (0, 128 - N_EXPERTS)))
    rhi = wr.astype(BF16)
    rlo = (wr - rhi.astype(F32)).astype(BF16)
    full = lambda shape: pl.BlockSpec(shape, lambda i: (0, 0))
    return pl.pallas_call(
        _router_kernel,
        grid=(T // tm,),
        in_specs=[pl.BlockSpec((tm, D_MODEL), lambda i: (i, 0)), full((1, D_MODEL)),
                  full((D_MODEL, 128)), full((D_MODEL, 128))],
        out_specs=pl.BlockSpec((tm, 128), lambda i: (i, 0)),
        out_shape=jax.ShapeDtypeStruct((T, 128), F32),
        compiler_params=_cp("parallel"),
        name="router",
    )(h, g[None, :], rhi, rlo)


MOE_TM = 512
MOE_TF = 1792
MOE_DISPATCH_TM = 512


def _row_copy(src_ref, src_row, dst_ref, dst_row, sem):
    return pltpu.make_async_copy(src_ref.at[pl.ds(src_row, 1)], dst_ref.at[pl.ds(dst_row, 1)], sem)


def _dispatch_kernel(pos1_ref, pos2_ref, h_ref, g_ref, xs_in_ref, xs_ref, hn_ref, sem):
    del xs_in_ref
    tm = h_ref.shape[0]
    base = pl.program_id(0) * tm
    hn_ref[...] = _rms(h_ref[...], g_ref[...])

    def issue(r, c):
        _row_copy(hn_ref, r, xs_ref, pos1_ref[base + r], sem).start()
        _row_copy(hn_ref, r, xs_ref, pos2_ref[base + r], sem).start()
        return c

    def drain(r, c):
        _row_copy(hn_ref, r, xs_ref, pos1_ref[base + r], sem).wait()
        _row_copy(hn_ref, r, xs_ref, pos2_ref[base + r], sem).wait()
        return c

    lax.fori_loop(0, tm, issue, 0, unroll=8)
    lax.fori_loop(0, tm, drain, 0, unroll=8)


def _moe_kernel(te_ref, nv_ref, xs_ref, wg_ref, wu_ref, wd_ref, y_ref, xb_ref, acc_ref):
    del te_ref
    i = pl.program_id(0)
    f = pl.program_id(1)

    @pl.when(i < nv_ref[0])
    def _():
        @pl.when(f == 0)
        def _():
            xb_ref[...] = xs_ref[...].astype(BF16)

        xb = xb_ref[...]
        a = _dot(xb, wg_ref[0])
        u = _dot(xb, wu_ref[0])
        t = _dot((a * _sigmoid(a) * u).astype(BF16), wd_ref[0])

        @pl.when(f == 0)
        def _():
            acc_ref[...] = t

        @pl.when(f > 0)
        def _():
            acc_ref[...] += t

        @pl.when(f == pl.num_programs(1) - 1)
        def _():
            y_ref[...] = acc_ref[...]

    @pl.when(i >= nv_ref[0])
    def _():
        y_ref[...] = jnp.zeros_like(y_ref)


def _combine_kernel(pos1_ref, pos2_ref, h_ref, rt_ref, fin_ref, y_ref, o_ref, ybuf_ref, sem):
    tm = h_ref.shape[0]
    base = pl.program_id(0) * tm

    def issue(r, c):
        _row_copy(y_ref, pos1_ref[base + r], ybuf_ref.at[0], r, sem).start()
        _row_copy(y_ref, pos2_ref[base + r], ybuf_ref.at[1], r, sem).start()
        return c

    def drain(r, c):
        _row_copy(y_ref, pos1_ref[base + r], ybuf_ref.at[0], r, sem).wait()
        _row_copy(y_ref, pos2_ref[base + r], ybuf_ref.at[1], r, sem).wait()
        return c

    lax.fori_loop(0, tm, issue, 0, unroll=8)
    lax.fori_loop(0, tm, drain, 0, unroll=8)
    rt = rt_ref[...]
    lane = lax.broadcasted_iota(jnp.int32, rt.shape, 1)
    w1 = jnp.sum(jnp.where(lane == 8, rt, 0.0), axis=-1, keepdims=True)
    w2 = jnp.sum(jnp.where(lane == 9, rt, 0.0), axis=-1, keepdims=True)
    o_ref[...] = _rms(h_ref[...] + (w1 * ybuf_ref[0] + w2 * ybuf_ref[1]), fin_ref[...])


def _moe_plan(rt, n_tiles):
    e1 = rt[:, 10].astype(jnp.int32)
    e2 = rt[:, 11].astype(jnp.int32)
    eid = jnp.arange(N_EXPERTS, dtype=jnp.int32)[None, :]
    m1 = (e1[:, None] == eid).astype(jnp.int32)
    m2 = (e2[:, None] == eid).astype(jnp.int32)
    m = m1 + m2
    counts = jnp.sum(m, axis=0)
    rank = jnp.cumsum(m, axis=0) - m
    padded = ((counts + MOE_TM - 1) // MOE_TM) * MOE_TM
    ends = jnp.cumsum(padded)
    pos_te = (ends - padded)[None, :] + rank
    pos1 = jnp.sum(m1 * pos_te, axis=1).astype(jnp.int32)
    pos2 = jnp.sum(m2 * pos_te, axis=1).astype(jnp.int32)
    nvalid = (ends[-1] // MOE_TM).astype(jnp.int32)
    starts = jnp.minimum(jnp.arange(n_tiles, dtype=jnp.int32), nvalid - 1) * MOE_TM
    tile_expert = jnp.sum((starts[:, None] >= ends[None, :]).astype(jnp.int32), axis=1)
    return pos1, pos2, tile_expert.astype(jnp.int32), nvalid.reshape(1)


def moe_final(h, g, rt, wg, wu, wd, final_g):
    T = h.shape[0]
    n_tiles = (2 * T) // MOE_TM + N_EXPERTS
    n_rows = n_tiles * MOE_TM
    nf = D_FF_EXPERT // MOE_TF
    pos1, pos2, tile_expert, nvalid = _moe_plan(rt, n_tiles)

    tm = MOE_DISPATCH_TM
    xs = pl.pallas_call(
        _dispatch_kernel,
        grid_spec=pltpu.PrefetchScalarGridSpec(
            num_scalar_prefetch=2,
            grid=(T // tm,),
            in_specs=[pl.BlockSpec((tm, D_MODEL), lambda i, p1, p2: (i, 0)),
                      pl.BlockSpec((1, D_MODEL), lambda i, p1, p2: (0, 0)),
                      pl.BlockSpec(memory_space=pl.ANY)],
            out_specs=pl.BlockSpec(memory_space=pl.ANY),
            scratch_shapes=[pltpu.VMEM((tm, D_MODEL), F32), pltpu.SemaphoreType.DMA(())]),
        out_shape=jax.ShapeDtypeStruct((n_rows, D_MODEL), F32),
        input_output_aliases={4: 0},
        compiler_params=_cp("arbitrary"),
        name="moe_dispatch",
    )(pos1, pos2, h, g[None, :], jnp.zeros((n_rows, D_MODEL), F32))

    def row_tile(i, f, te, nv):
        return (jnp.minimum(i, nv[0] - 1), 0)

    def w_up(i, f, te, nv):
        return (te[i], 0, jnp.where(i < nv[0], f, nf - 1))

    def w_down(i, f, te, nv):
        return (te[i], jnp.where(i < nv[0], f, nf - 1), 0)

    y = pl.pallas_call(
        _moe_kernel,
        grid_spec=pltpu.PrefetchScalarGridSpec(
            num_scalar_prefetch=2,
            grid=(n_tiles, nf),
            in_specs=[pl.BlockSpec((MOE_TM, D_MODEL), row_tile),
                      pl.BlockSpec((1, D_MODEL, MOE_TF), w_up),
                      pl.BlockSpec((1, D_MODEL, MOE_TF), w_up),
                      pl.BlockSpec((1, MOE_TF, D_MODEL), w_down)],
            out_specs=pl.BlockSpec((MOE_TM, D_MODEL), lambda i, f, te, nv: (i, 0)),
            scratch_shapes=[pltpu.VMEM((MOE_TM, D_MODEL), BF16), pltpu.VMEM((MOE_TM, D_MODEL), F32)]),
        out_shape=jax.ShapeDtypeStruct((n_rows, D_MODEL), F32),
        compiler_params=_cp("arbitrary", "arbitrary"),
        name="moe_experts",
    )(tile_expert, nvalid, xs, wg.astype(BF16), wu.astype(BF16), wd.astype(BF16))

    return pl.pallas_call(
        _combine_kernel,
        grid_spec=pltpu.PrefetchScalarGridSpec(
            num_scalar_prefetch=2,
            grid=(T // tm,),
            in_specs=[pl.BlockSpec((tm, D_MODEL), lambda i, p1, p2: (i, 0)),
                      pl.BlockSpec((tm, 128), lambda i, p1, p2: (i, 0)),
                      pl.BlockSpec((1, D_MODEL), lambda i, p1, p2: (0, 0)),
                      pl.BlockSpec(memory_space=pl.ANY)],
            out_specs=pl.BlockSpec((tm, D_MODEL), lambda i, p1, p2: (i, 0)),
            scratch_shapes=[pltpu.VMEM((2, tm, D_MODEL), F32), pltpu.SemaphoreType.DMA(())]),
        out_shape=jax.ShapeDtypeStruct((T, D_MODEL), F32),
        compiler_params=_cp("arbitrary"),
        name="moe_combine",
    )(pos1, pos2, h, rt, final_g[None, :], y)


def _final_norm_kernel(h_ref, g_ref, o_ref):
    o_ref[...] = _rms(h_ref[...], g_ref[...])


def final_norm(h, g):
    T = h.shape[0]
    tm = 1024
    return pl.pallas_call(
        _final_norm_kernel,
        grid=(T // tm,),
        in_specs=[pl.BlockSpec((tm, D_MODEL), lambda i: (i, 0)), pl.BlockSpec((1, D_MODEL), lambda i: (0, 0))],
        out_specs=pl.BlockSpec((tm, D_MODEL), lambda i: (i, 0)),
        out_shape=jax.ShapeDtypeStruct((T, D_MODEL), F32),
        compiler_params=_cp("parallel"),
        name="final_norm",
    )(h, g[None, :])


def kernel(x, positions, attn_norm, w_in, rwkv_mu, rwkv_w0, rwkv_w2, rwkv_a0, rwkv_a2, rwkv_g2, rwkv_k_k, rwkv_k_a,
           rwkv_r_k, rwkv_lnx_w, mla_q_norm, mla_w_qb, mla_kv_norm, mla_w_kvb, w_branch, w_o, ffn_norm, ffn_w_gate,
           ffn_w_up, ffn_w_down, moe_router, moe_w_gate, moe_w_up, moe_w_down, final_norm_g):
    B, S, D = x.shape
    T = B * S
    depth = attn_norm.shape[0]
    rwkv_all = dict(rwkv_mu=rwkv_mu, rwkv_w0=rwkv_w0, rwkv_w2=rwkv_w2, rwkv_a0=rwkv_a0, rwkv_a2=rwkv_a2,
                    rwkv_g2=rwkv_g2, rwkv_k_k=rwkv_k_k, rwkv_k_a=rwkv_k_a, rwkv_r_k=rwkv_r_k,
                    rwkv_lnx_w=rwkv_lnx_w)
    tab = rope_tables(positions)
    h = x.reshape(T, D)
    out = None
    for l in range(depth):
        p = norm_in_proj(h, attn_norm[l][None, :], relayout_w_in(w_in[l]))
        y_ret = retention(p, tab, B, S)
        y_rwkv = rwkv7(p, {k: v[l] for k, v in rwkv_all.items()}, B, S)
        y_mla = mla(p, tab, mla_q_norm[l], mla_w_qb[l], mla_kv_norm[l], mla_w_kvb[l], B, S)
        h = merge(y_ret, y_rwkv, y_mla, p, h, w_branch[l], w_o[l])
        last = l == depth - 1
        if l % 2 == 0:
            h = dense_ffn(h, ffn_norm[l], ffn_w_gate[l // 2], ffn_w_up[l // 2], ffn_w_down[l // 2])
            if last:
                out = final_norm(h, final_norm_g)
        else:
            rt = router(h, ffn_norm[l], moe_router[l // 2])
            out_l = moe_final(h, ffn_norm[l], rt, moe_w_gate[l // 2], moe_w_up[l // 2], moe_w_down[l // 2],
                              final_norm_g)
            if last:
                out = out_l
            else:
                raise NotImplementedError("a MoE layer that is not the last layer")
    return out.reshape(B, S, D)
```

```python
import functools

import numpy as np
import jax
import jax.numpy as jnp
from jax import lax
from jax.experimental import pallas as pl
from jax.experimental.pallas import tpu as pltpu

F32 = jnp.float32
BF16 = jnp.bfloat16

D_MODEL = 1024
CHUNK = 64
NORM_EPS = 1e-6
ROPE_BASE = 10000.0

RET_HEADS = 4
RET_DK = 128
RET_GN_EPS = 1e-5

RWKV_HEAD = 64
RWKV_HEADS = 8
RWKV_W = 512
RWKV_GN_EPS = 64e-5
RWKV_COLS = 1824

MLA_HEADS = 8
MLA_Q_LORA = 256
MLA_KV_LORA = 128
MLA_NOPE = 64
MLA_ROPE = 32
MLA_V = 64

N_EXPERTS = 8
D_FF = 2816
D_FF_EXPERT = 3584

P_RET = 0
P_RWKV = 2048
P_GATE = 4096
P_MLA = 7168
P_COLS = 7680

BLK = 256
VMEM_LIMIT_BYTES = 56 * 1024 * 1024


def _cp(*sem):
    return pltpu.CompilerParams(dimension_semantics=sem, vmem_limit_bytes=VMEM_LIMIT_BYTES)


def _sigmoid(z):
    return 1.0 / (1.0 + jnp.exp(-z))


def _split3(x):
    hi = x.astype(BF16)
    r1 = x - hi.astype(F32)
    mid = r1.astype(BF16)
    lo = (r1 - mid.astype(F32)).astype(BF16)
    return hi, mid, lo


def _dot(a, b):
    return jnp.dot(a, b, preferred_element_type=F32)


def _dot_nt(a, b):
    return lax.dot_general(a, b, (((1,), (1,)), ((), ())), preferred_element_type=F32)


def _dot_tn(a, b):
    return lax.dot_general(a, b, (((0,), (0,)), ((), ())), preferred_element_type=F32)


def _dot_exact_lhs(m_bf16, x_f32):
    hi, mid, lo = _split3(x_f32)
    return _dot(m_bf16, hi) + _dot(m_bf16, mid) + _dot(m_bf16, lo)


def _dot_exact_rhs(x_f32, m_bf16):
    hi, mid, lo = _split3(x_f32)
    return _dot(hi, m_bf16) + _dot(mid, m_bf16) + _dot(lo, m_bf16)


def _dot_split2_rhs(x_f32, m_bf16):
    hi, mid, _ = _split3(x_f32)
    return _dot(hi, m_bf16) + _dot(mid, m_bf16)


TAB_COLS = 768


def _tables_kernel(pos_ref, freq_ref, perm_ref, o_ref):
    ang = pos_ref[...] * freq_ref[...]
    cs = jnp.concatenate([jnp.cos(ang), jnp.sin(ang)], axis=1)
    o_ref[...] = _dot_exact_rhs(cs, perm_ref[...])


def _table_perm():
    p = np.zeros((256, TAB_COLS), np.float32)
    c, s = 0, 128
    for j in range(64):
        p[c + j, j] = 1.0
        p[c + j, 64 + j] = 1.0
        p[s + j, 128 + j] = -1.0
        p[s + j, 128 + 64 + j] = 1.0
    for j in range(64):
        p[c + 127, 256 + j] = 1.0
    for j in range(16):
        cj, sj = c + 64 + j, s + 64 + j
        q0 = 256 + 64
        p[cj, q0 + j] = 1.0
        p[sj, q0 + 16 + j] = 1.0
        p[sj, q0 + 32 + j] = 1.0
        p[cj, q0 + 48 + j] = 1.0
        k0 = 384
        p[cj, k0 + j] = 1.0
        p[cj, k0 + 16 + j] = -1.0
        p[sj, k0 + 32 + j] = 1.0
        p[sj, k0 + 48 + j] = 1.0
        p[sj, k0 + 64 + j] = -1.0
        p[sj, k0 + 80 + j] = 1.0
        p[cj, k0 + 96 + j] = 1.0
        p[cj, k0 + 112 + j] = 1.0
    return p


def rope_tables(positions):
    T = positions.size
    tm = 512
    inv_ret = 1.0 / (ROPE_BASE ** (jnp.arange(0, RET_DK, 2, dtype=F32) / RET_DK))
    inv_mla = 1.0 / (ROPE_BASE ** (jnp.arange(0, MLA_ROPE, 2, dtype=F32) / MLA_ROPE))
    freq = jnp.concatenate([inv_ret, inv_mla, jnp.zeros((48,), F32)])[None, :]
    pos = positions.reshape(T, 1).astype(F32)
    perm = jnp.asarray(_table_perm(), BF16)
    return pl.pallas_call(
        _tables_kernel,
        grid=(T // tm,),
        in_specs=[pl.BlockSpec((tm, 1), lambda i: (i, 0)),
                  pl.BlockSpec((1, 128), lambda i: (0, 0)),
                  pl.BlockSpec((256, TAB_COLS), lambda i: (0, 0))],
        out_specs=pl.BlockSpec((tm, TAB_COLS), lambda i: (i, 0)),
        out_shape=jax.ShapeDtypeStruct((T, TAB_COLS), F32),
        compiler_params=_cp("parallel"),
        name="rope_tables",
    )(pos, freq, perm)


def _norm_matmul_kernel(x_ref, g_ref, w_ref, o_ref, hn_ref):
    @pl.when(pl.program_id(1) == 0)
    def _():
        x = x_ref[...]
        ms = jnp.mean(x * x, axis=-1, keepdims=True)
        hn_ref[...] = (x * lax.rsqrt(ms + NORM_EPS) * g_ref[...]).astype(BF16)

    o_ref[...] = _dot(hn_ref[...], w_ref[...]).astype(o_ref.dtype)


def norm_in_proj(h, g, w):
    T = h.shape[0]
    n = w.shape[1]
    tm, tn = min(2048, T), 1536
    return pl.pallas_call(
        _norm_matmul_kernel,
        grid=(T // tm, n // tn),
        in_specs=[pl.BlockSpec((tm, D_MODEL), lambda i, j: (i, 0)),
                  pl.BlockSpec((1, D_MODEL), lambda i, j: (0, 0)),
                  pl.BlockSpec((D_MODEL, tn), lambda i, j: (0, j))],
        out_specs=pl.BlockSpec((tm, tn), lambda i, j: (i, j)),
        out_shape=jax.ShapeDtypeStruct((T, n), BF16),
        scratch_shapes=[pltpu.VMEM((tm, D_MODEL), BF16)],
        compiler_params=_cp("parallel", "arbitrary"),
        name="norm_in_proj",
    )(h, g, w)


def relayout_w_in(w):
    d = w.shape[0]
    kpe = 3872 + 256 + 128
    return jnp.concatenate([
        w[:, :3872],
        jnp.zeros((d, P_GATE - 3872), w.dtype),
        w[:, 4288:7360],
        w[:, 3872:4256],
        jnp.tile(w[:, kpe:kpe + 16], (1, 4)),
        jnp.tile(w[:, kpe + 16:kpe + 32], (1, 4)),
    ], axis=1).astype(BF16)


def _retention_kernel(q_ref, k_ref, v_ref, g_ref, cos_ref, sin_ref, mask_ref, qd_ref, kd_ref, cd_ref,
                      o_ref, st_ref):
    @pl.when(pl.program_id(1) == 0)
    def _():
        st_ref[...] = jnp.zeros_like(st_ref)

    cos2 = cos_ref[...]
    sin2 = sin_ref[...]
    for h in range(RET_HEADS):
        sl = slice(h * RET_DK, (h + 1) * RET_DK)
        q = q_ref[:, sl].astype(F32)
        k = k_ref[:, sl].astype(F32)
        q = q * cos2 + pltpu.roll(q, 64, 1) * sin2
        k = (k * cos2 + pltpu.roll(k, 64, 1) * sin2) * (RET_DK ** -0.5)
        v = v_ref[:, sl]
        qb = q.astype(BF16)
        kb = k.astype(BF16)
        scores = _dot_nt(qb, kb) * mask_ref[h]
        o = _dot(scores.astype(BF16), v)
        st = st_ref[h]
        o = o + _dot((q * qd_ref[h]).astype(BF16), st.astype(BF16))
        st_ref[h] = st * cd_ref[h] + _dot_tn((k * kd_ref[h]).astype(BF16), v)
        mean = jnp.mean(o, axis=-1, keepdims=True)
        oc = o - mean
        var = jnp.mean(oc * oc, axis=-1, keepdims=True)
        y = oc * lax.rsqrt(var + RET_GN_EPS)
        g = g_ref[:, sl].astype(F32)
        o_ref[:, sl] = (y * (g * _sigmoid(g))).astype(o_ref.dtype)


def _retention_consts():
    hh = jnp.arange(RET_HEADS, dtype=F32)
    log_gamma = jnp.log(1.0 - 2.0 ** (-5.0 - hh))
    idx = jnp.arange(BLK, dtype=F32)
    dist = jnp.abs(idx[:, None] - idx[None, :])
    ci = np.arange(BLK) // CHUNK
    visible = jnp.asarray(ci[None, :] <= ci[:, None])
    mask = jnp.where(visible[None], jnp.exp(log_gamma[:, None, None] * dist[None]), 0.0)
    qd = jnp.exp(log_gamma[:, None] * (idx[None, :] + 1.0))
    kd = jnp.exp(log_gamma[:, None] * (BLK - 1.0 - idx[None, :]))
    cd = jnp.exp(log_gamma * BLK)
    qd = jnp.broadcast_to(qd[:, :, None], (RET_HEADS, BLK, RET_DK))
    kd = jnp.broadcast_to(kd[:, :, None], (RET_HEADS, BLK, RET_DK))
    cd = jnp.broadcast_to(cd[:, None, None], (RET_HEADS, 1, RET_DK))
    return mask, qd, kd, cd


def retention(p, tab, B, S):
    T = B * S
    nb = S // BLK
    mask, qd, kd, cd = _retention_consts()
    row = lambda c: pl.BlockSpec((BLK, 512), lambda b, i, c=c: (b * nb + i, c))
    tcol = lambda c: pl.BlockSpec((BLK, 128), lambda b, i, c=c: (b * nb + i, c))
    const3 = lambda shape: pl.BlockSpec(shape, lambda b, i: (0, 0, 0))
    return pl.pallas_call(
        _retention_kernel,
        grid=(B, nb),
        in_specs=[row(0), row(1), row(2), row(3), tcol(0), tcol(1),
                  const3((RET_HEADS, BLK, BLK)), const3((RET_HEADS, BLK, RET_DK)),
                  const3((RET_HEADS, BLK, RET_DK)), const3((RET_HEADS, 1, RET_DK))],
        out_specs=pl.BlockSpec((BLK, 512), lambda b, i: (b * nb + i, 0)),
        out_shape=jax.ShapeDtypeStruct((T, 512), BF16),
        scratch_shapes=[pltpu.VMEM((RET_HEADS, RET_DK, RET_DK), F32)],
        compiler_params=_cp("arbitrary", "arbitrary"),
        name="retention",
    )(p, p, p, p, tab, tab, mask, qd, kd, cd)


def _rwkv_pre_kernel(p_ref, mu_ref, w0_ref, w2_ref, a0_ref, a2_ref, g2_ref, kk_w_ref, ka_ref, rk_ref, seg_ref,
                     r_o, w_o, k_o, v_o, kk_o, b_o, bonus_o, g_o, prev_ref):
    tm = p_ref.shape[0]

    @pl.when(pl.program_id(1) == 0)
    def _():
        prev_ref[...] = jnp.zeros_like(prev_ref)

    x = p_ref[...].astype(F32)
    row = lax.broadcasted_iota(jnp.int32, x.shape, 0)
    shifted = jnp.where(row == 0, prev_ref[7:8, :], pltpu.roll(x, 1, 0))
    prev_ref[...] = x[tm - 8:tm, :]
    xs = x + (shifted - x) * mu_ref[...]
    r = xs[:, 0:512]
    k = xs[:, 512:1024]
    v = xs[:, 1024:1536]
    xwa = xs[:, 1536:1664]
    xg = xs[:, 1664:1920]
    z = w0_ref[...] + _dot(jnp.tanh(xwa).astype(BF16), w2_ref[...])
    logw = -_sigmoid(z) * float(np.exp(-0.5))
    a = _sigmoid(a0_ref[...] + _dot(xwa.astype(BF16), a2_ref[...]))
    g = _dot(_sigmoid(xg).astype(BF16), g2_ref[...])
    seg = seg_ref[...]
    kk = k * kk_w_ref[...]
    kk = kk * lax.rsqrt(_dot_split2_rhs(kk * kk, seg) + 1e-12)
    k = k * (1.0 + (a - 1.0) * ka_ref[...])
    bonus = _dot_split2_rhs(r * k * rk_ref[...], seg) * v
    r_o[...] = r.astype(r_o.dtype)
    w_o[...] = logw
    k_o[...] = k.astype(k_o.dtype)
    v_o[...] = v.astype(v_o.dtype)
    kk_o[...] = kk.astype(kk_o.dtype)
    b_o[...] = (kk * a).astype(b_o.dtype)
    bonus_o[...] = bonus.astype(bonus_o.dtype)
    g_o[...] = g.astype(g_o.dtype)


RWKV_PAIRS_PER_STEP = 4


def _rwkv_rec_kernel(r_ref, w_ref, k_ref, v_ref, kk_ref, b_ref, bonus_ref, g_ref, lnx_ref, o_ref, st_ref, y_ref):
    @pl.when(pl.program_id(2) == 0)
    def _():
        st_ref[...] = jnp.zeros_like(st_ref)

    row = lax.broadcasted_iota(jnp.int32, (BLK, BLK), 0)
    col = lax.broadcasted_iota(jnp.int32, (BLK, BLK), 1)
    same = (row >> 6) == (col >> 6)
    incl = same & (col <= row)
    strict = same & (col < row)
    tri = jnp.where(incl, 1.0, 0.0).astype(BF16)
    lane = lax.broadcasted_iota(jnp.int32, (BLK, 128), 1)
    head0 = lane < RWKV_HEAD
    rk_ = lax.broadcasted_iota(jnp.int32, (128, 128), 0)
    ck_ = lax.broadcasted_iota(jnp.int32, (128, 128), 1)
    bd = (rk_ >> 6) == (ck_ >> 6)
    n_chunks = BLK // CHUNK
    pairs = range(RWKV_PAIRS_PER_STEP)
    lanes = [slice(pi * 128, (pi + 1) * 128) for pi in pairs]

    vb, rt, kkt, yt, bh, kh, wc = [], [], [], [], [], [], []
    cum_all = _dot_exact_lhs(tri, w_ref[...])
    for ps in lanes:
        r = r_ref[:, ps].astype(F32)
        k = k_ref[:, ps].astype(F32)
        kk = kk_ref[:, ps].astype(F32)
        b = b_ref[:, ps].astype(F32)
        logw = w_ref[:, ps]
        cum = cum_all[:, ps]
        tot = jnp.concatenate([jnp.broadcast_to(cum[(c + 1) * CHUNK - 1:(c + 1) * CHUNK, :], (CHUNK, 128))
                               for c in range(n_chunks)], axis=0)
        e_neg = jnp.exp(-cum)
        e_end = jnp.exp(tot - cum)
        vb.append(v_ref[:, ps])
        rt.append(r * jnp.exp(cum))
        kkt.append(kk * jnp.exp(cum - logw))
        yt.append(jnp.concatenate([b * e_neg, k * e_neg], axis=0).astype(BF16))
        bh.append(b * e_end)
        kh.append(k * e_end)
        wc.append(jnp.exp(tot))

    streams = [(pi, h) for pi in pairs for h in range(2)]
    apow, xs, lrb, lrk = [], [], [], []
    for pi, h in streams:
        mh = head0 if h == 0 else jnp.logical_not(head0)
        xh = jnp.concatenate([jnp.where(mh, kkt[pi], 0.0), jnp.where(mh, rt[pi], 0.0)], axis=0).astype(BF16)
        gm = _dot_nt(xh, yt[pi])
        apow.append(jnp.where(strict, gm[:BLK, :BLK], 0.0))
        a_uk = jnp.where(strict, gm[:BLK, BLK:], 0.0)
        lrb.append(jnp.where(incl, gm[BLK:, :BLK], 0.0).astype(BF16))
        lrk.append(jnp.where(incl, gm[BLK:, BLK:], 0.0).astype(BF16))
        xs.append(jnp.concatenate([kkt[pi], _dot(a_uk.astype(BF16), vb[pi])], axis=1))
    xs = [x - _dot(a.astype(BF16), x.astype(BF16)) for a, x in zip(apow, xs)]
    for _ in range(5):
        apow = [_dot(a.astype(BF16), a.astype(BF16)) for a in apow]
        xs = [x + _dot(a.astype(BF16), x.astype(BF16)) for a, x in zip(apow, xs)]

    pm, qm, rp, y0 = [], [], [], []
    for pi in pairs:
        x0, x1 = xs[2 * pi], xs[2 * pi + 1]
        pm.append(jnp.where(head0, x0[:, :128], x1[:, :128]))
        qm.append(jnp.where(head0, x0[:, 128:], x1[:, 128:]))
    for pi in pairs:
        pq = jnp.concatenate([pm[pi], qm[pi]], axis=1).astype(BF16)
        lp0 = _dot(lrb[2 * pi], pq)
        lp1 = _dot(lrb[2 * pi + 1], pq)
        rp.append(rt[pi] - jnp.where(head0, lp0[:, :128], lp1[:, :128]))
        y0.append(jnp.where(head0, _dot(lrk[2 * pi], vb[pi]) - lp0[:, 128:],
                            _dot(lrk[2 * pi + 1], vb[pi]) - lp1[:, 128:]))

    ptb = [[None] * n_chunks for _ in pairs]
    nn = [[None] * n_chunks for _ in pairs]
    for c in range(n_chunks):
        sl = slice(c * CHUNK, (c + 1) * CHUNK)
        for pi in pairs:
            lhs = jnp.concatenate([pm[pi][sl], qm[pi][sl], vb[pi][sl].astype(F32)], axis=1).astype(BF16)
            rhs = jnp.concatenate([bh[pi][sl], kh[pi][sl]], axis=1).astype(BF16)
            zz = _dot_tn(lhs, rhs)
            ptb[pi][c] = jnp.where(bd, zz[0:128, 0:128], 0.0).astype(BF16)
            nn[pi][c] = jnp.where(bd, zz[256:384, 128:256] - zz[128:256, 0:128], 0.0)

    st = [st_ref[pi] for pi in pairs]
    for c in range(n_chunks):
        sl = slice(c * CHUNK, (c + 1) * CHUNK)
        for pi in pairs:
            stb = st[pi].astype(BF16)
            y_ref[sl, lanes[pi]] = _dot_nt(rp[pi][sl].astype(BF16), stb) + y0[pi][sl]
            st[pi] = st[pi] * wc[pi][c * CHUNK:c * CHUNK + 1, :] - _dot(stb, ptb[pi][c]) + nn[pi][c]
    for pi in pairs:
        st_ref[pi] = st[pi]

    def head_mean(x):
        s0 = jnp.sum(jnp.where(head0, x, 0.0), axis=-1, keepdims=True)
        s1 = jnp.sum(x, axis=-1, keepdims=True) - s0
        return jnp.where(head0, s0, s1) * (1.0 / RWKV_HEAD)

    for ps in lanes:
        y = y_ref[:, ps]
        yc = y - head_mean(y)
        var = head_mean(yc * yc)
        yn = yc * lax.rsqrt(var + RWKV_GN_EPS) * lnx_ref[:, ps]
        o_ref[:, ps] = ((yn + bonus_ref[:, ps].astype(F32)) * g_ref[:, ps].astype(F32)).astype(o_ref.dtype)


def rwkv7(p, prm, B, S):
    T = B * S
    tm = 256
    nb = S // tm
    pad1 = lambda a, n: jnp.pad(a.astype(F32), (0, n - a.shape[0]))[None, :]
    mu = pad1(prm['rwkv_mu'], 2048)
    w2 = jnp.zeros((128, RWKV_W), F32).at[0:64].set(prm['rwkv_w2']).astype(BF16)
    a2 = jnp.zeros((128, RWKV_W), F32).at[64:128].set(prm['rwkv_a2']).astype(BF16)
    g2 = jnp.zeros((256, RWKV_W), F32).at[0:160].set(prm['rwkv_g2']).astype(BF16)
    hid = np.arange(RWKV_W) // RWKV_HEAD
    seg = jnp.asarray(hid[:, None] == hid[None, :], BF16)
    row1 = lambda a: a.reshape(1, RWKV_W).astype(F32)
    vec = pl.BlockSpec((1, RWKV_W), lambda b, i: (0, 0))
    out_blk = pl.BlockSpec((tm, RWKV_W), lambda b, i: (b * nb + i, 0))
    sds = lambda dt: jax.ShapeDtypeStruct((T, RWKV_W), dt)
    r, logw, k, v, kk, bb, bonus, g = pl.pallas_call(
        _rwkv_pre_kernel,
        grid=(B, nb),
        in_specs=[pl.BlockSpec((tm, 2048), lambda b, i: (b * nb + i, 1)),
                  pl.BlockSpec((1, 2048), lambda b, i: (0, 0)),
                  vec, pl.BlockSpec((128, RWKV_W), lambda b, i: (0, 0)),
                  vec, pl.BlockSpec((128, RWKV_W), lambda b, i: (0, 0)),
                  pl.BlockSpec((256, RWKV_W), lambda b, i: (0, 0)),
                  vec, vec, vec,
                  pl.BlockSpec((RWKV_W, RWKV_W), lambda b, i: (0, 0))],
        out_specs=[out_blk] * 8,
        out_shape=[sds(BF16), sds(F32), sds(BF16), sds(BF16), sds(BF16), sds(BF16), sds(BF16), sds(BF16)],
        scratch_shapes=[pltpu.VMEM((8, 2048), F32)],
        compiler_params=_cp("arbitrary", "arbitrary"),
        name="rwkv_pre",
    )(p, mu, row1(prm['rwkv_w0']), w2, row1(prm['rwkv_a0']), a2, g2,
      row1(prm['rwkv_k_k']), row1(prm['rwkv_k_a']), row1(prm['rwkv_r_k']), seg)

    nblk = S // BLK
    lanes = 128 * RWKV_PAIRS_PER_STEP
    blk = pl.BlockSpec((BLK, lanes), lambda b, j, i: (b * nblk + i, j))
    return pl.pallas_call(
        _rwkv_rec_kernel,
        grid=(B, RWKV_W // lanes, nblk),
        in_specs=[blk] * 8 + [pl.BlockSpec((1, lanes), lambda b, j, i: (0, j))],
        out_specs=blk,
        out_shape=jax.ShapeDtypeStruct((T, RWKV_W), BF16),
        scratch_shapes=[pltpu.VMEM((RWKV_PAIRS_PER_STEP, 128, 128), F32), pltpu.VMEM((BLK, lanes), F32)],
        compiler_params=_cp("arbitrary", "arbitrary", "arbitrary"),
        name="rwkv_rec",
    )(r, logw, k, v, kk, bb, bonus, g, row1(prm['rwkv_lnx_w']))


MLA_TQ = 1024
MLA_TK = 512
MLA_RQ = 256
MLA_VT = 80


def _mla_prep_kernel(p_ref, tq_ref, tk_ref, qn_ref, wq_ref, kn_ref, wk_ref, wv_ref, q_o, k_o, v_o):
    x = p_ref[...].astype(F32)
    qa = x[:, 0:256]
    ckv = x[:, 256:384]
    kpe = x[:, 384:512]
    qn = (qa * lax.rsqrt(jnp.mean(qa * qa, axis=-1, keepdims=True) + NORM_EPS) * qn_ref[...]).astype(BF16)
    cn = (ckv * lax.rsqrt(jnp.mean(ckv * ckv, axis=-1, keepdims=True) + NORM_EPS) * kn_ref[...]).astype(BF16)
    q = _dot(qn, wq_ref[...])
    kn = _dot(cn, wk_ref[...])
    vt = _dot_nt(wv_ref[...], cn)
    vrow = lax.rem(lax.broadcasted_iota(jnp.int32, vt.shape, 0), MLA_VT)
    v_o[...] = jnp.where(vrow >= MLA_V, 1.0, vt).astype(v_o.dtype)
    prod = kpe * tk_ref[...]
    lane = lax.broadcasted_iota(jnp.int32, prod.shape, 1)
    kr = jnp.where(lane >= 64, prod + pltpu.roll(prod, 64, 1), 0.0)
    tq = tq_ref[...]
    for h in range(MLA_HEADS):
        sl = slice(h * 128, (h + 1) * 128)
        q_o[:, sl] = (q[:, sl] * tq).astype(q_o.dtype)
        k_o[:, sl] = (kn[:, sl] + kr).astype(k_o.dtype)


def _mla_flash_kernel(q_ref, k_ref, v_ref, o_ref):
    i = pl.program_id(2)
    tq, tk, rq = MLA_TQ, MLA_TK, MLA_RQ
    keyc = lax.broadcasted_iota(jnp.int32, (tk, tk), 0) >> 6
    qryc = lax.broadcasted_iota(jnp.int32, (tk, tk), 1) >> 6
    diag_mask = keyc <= qryc

    chains = [(h, r0) for h in range(2) for r0 in range(0, tq, rq)]

    def tiles(off, carry, first_diag_row):
        def mask_of(r0):
            if first_diag_row is None or r0 >= first_diag_row + tk:
                return "all"
            if r0 < first_diag_row:
                return "none"
            return diag_mask[:, r0 - first_diag_row:r0 - first_diag_row + rq]

        active = [c for c, (_, r0) in enumerate(chains) if not isinstance(mask_of(r0), str) or mask_of(r0) == "all"]
        scores = {}
        for c in active:
            h, r0 = chains[c]
            hs = slice(h * 128, (h + 1) * 128)
            s = _dot_nt(k_ref[pl.ds(off, tk), hs], q_ref[r0:r0 + rq, hs])
            mk = mask_of(r0)
            if not isinstance(mk, str):
                s = jnp.where(mk, s, -1e30)
            scores[c] = s.astype(BF16)
        probs = {}
        for c in active:
            m = carry[c][0]
            m_new = jnp.maximum(m, jnp.max(scores[c], axis=0, keepdims=True).astype(F32))
            probs[c] = (m_new, jnp.exp(m - m_new), jnp.exp(scores[c] - m_new.astype(BF16)))
        out = list(carry)
        for c in active:
            h, r0 = chains[c]
            m_new, alpha, pexp = probs[c]
            vt = v_ref[h * MLA_VT:(h + 1) * MLA_VT, pl.ds(off, tk)]
            out[c] = (m_new, alpha * carry[c][1] + _dot(vt, pexp))
        return tuple(out)

    def body(j, carry):
        for d in range(tq // tk):
            carry = tiles(pl.multiple_of(j * tq + d * tk, tk), carry, None)
        return carry

    init = tuple((jnp.full((1, rq), -1e30, F32), jnp.zeros((MLA_VT, rq), F32)) for _ in chains)
    carry = lax.fori_loop(0, i, body, init)
    for d in range(tq // tk):
        carry = tiles(pl.multiple_of(i * tq + d * tk, tk), carry, d * tk)
    per_head = len(chains) // 2
    heads = []
    for h in range(2):
        acc = jnp.concatenate([c[1] for c in carry[h * per_head:(h + 1) * per_head]], axis=1)
        heads.append(acc[:MLA_V] / acc[MLA_V:MLA_V + 1])
    o_ref[...] = jnp.concatenate(heads, axis=0).T.astype(o_ref.dtype)


def mla(p, tab, q_norm, w_qb, kv_norm, w_kvb, B, S):
    T = B * S
    tm = 512
    scale = (MLA_NOPE + MLA_ROPE) ** -0.5
    wq3 = (w_qb * scale).reshape(MLA_Q_LORA, MLA_HEADS, MLA_NOPE + MLA_ROPE)
    x1 = wq3[:, :, MLA_NOPE:MLA_NOPE + 16]
    x2 = wq3[:, :, MLA_NOPE + 16:]
    wq = jnp.concatenate([wq3[:, :, :MLA_NOPE], x1, x2, x1, x2], axis=2).reshape(MLA_Q_LORA, 1024).astype(BF16)
    wkv3 = w_kvb.reshape(MLA_KV_LORA, MLA_HEADS, MLA_NOPE + MLA_V)
    wk = jnp.concatenate([wkv3[:, :, :MLA_NOPE], jnp.zeros((MLA_KV_LORA, MLA_HEADS, 64), w_kvb.dtype)],
                         axis=2).reshape(MLA_KV_LORA, 1024).astype(BF16)
    wv = jnp.concatenate([wkv3[:, :, MLA_NOPE:], jnp.zeros((MLA_KV_LORA, MLA_HEADS, MLA_VT - MLA_V), w_kvb.dtype)],
                         axis=2).reshape(MLA_KV_LORA, MLA_HEADS * MLA_VT).T.astype(BF16)
    full = lambda shape: pl.BlockSpec(shape, lambda i: (0, 0))
    q, k, v = pl.pallas_call(
        _mla_prep_kernel,
        grid=(T // tm,),
        in_specs=[pl.BlockSpec((tm, 512), lambda i: (i, P_MLA // 512)),
                  pl.BlockSpec((tm, 128), lambda i: (i, 2)),
                  pl.BlockSpec((tm, 128), lambda i: (i, 3)),
                  full((1, 256)), full((256, 1024)), full((1, 128)), full((128, 1024)),
                  full((MLA_HEADS * MLA_VT, 128))],
        out_specs=[pl.BlockSpec((tm, 1024), lambda i: (i, 0)),
                   pl.BlockSpec((tm, 1024), lambda i: (i, 0)),
                   pl.BlockSpec((MLA_HEADS * MLA_VT, tm), lambda i: (0, i))],
        out_shape=[jax.ShapeDtypeStruct((T, 1024), BF16), jax.ShapeDtypeStruct((T, 1024), BF16),
                   jax.ShapeDtypeStruct((MLA_HEADS * MLA_VT, T), BF16)],
        compiler_params=_cp("parallel"),
        name="mla_prep",
    )(p, tab, tab, q_norm[None, :].astype(F32), wq, kv_norm[None, :].astype(F32), wk, wv)

    nq = S // MLA_TQ
    return pl.pallas_call(
        _mla_flash_kernel,
        grid=(B, MLA_HEADS // 2, nq),
        in_specs=[pl.BlockSpec((MLA_TQ, 256), lambda b, j, i: (b * nq + i, j)),
                  pl.BlockSpec((S, 256), lambda b, j, i: (b, j)),
                  pl.BlockSpec((2 * MLA_VT, S), lambda b, j, i: (j, b))],
        out_specs=pl.BlockSpec((MLA_TQ, 128), lambda b, j, i: (b * nq + i, j)),
        out_shape=jax.ShapeDtypeStruct((T, 512), BF16),
        compiler_params=_cp("parallel", "parallel", "arbitrary"),
        name="mla_flash",
    )(q, k, v)


def _merge_kernel(yr_ref, yw_ref, ym_ref, g0_ref, g1_ref, g2_ref, h_ref, wb_ref, wo_ref, o_ref):
    merged = None
    for n, (y_ref, g_ref) in enumerate(((yr_ref, g0_ref), (yw_ref, g1_ref), (ym_ref, g2_ref))):
        term = _sigmoid(g_ref[...].astype(F32)) * _dot(y_ref[...], wb_ref[n])
        merged = term if merged is None else merged + term
    o_ref[...] = h_ref[...] + _dot(merged.astype(BF16), wo_ref[...])


def merge(y_ret, y_rwkv, y_mla, p, h, w_branch, w_o):
    T = h.shape[0]
    tm = 512
    yb = pl.BlockSpec((tm, 512), lambda i: (i, 0))
    gate = lambda n: pl.BlockSpec((tm, D_MODEL), lambda i, n=n: (i, P_GATE // D_MODEL + n))
    return pl.pallas_call(
        _merge_kernel,
        grid=(T // tm,),
        in_specs=[yb, yb, yb, gate(0), gate(1), gate(2),
                  pl.BlockSpec((tm, D_MODEL), lambda i: (i, 0)),
                  pl.BlockSpec((3, 512, D_MODEL), lambda i: (0, 0, 0)),
                  pl.BlockSpec((D_MODEL, D_MODEL), lambda i: (0, 0))],
        out_specs=pl.BlockSpec((tm, D_MODEL), lambda i: (i, 0)),
        out_shape=jax.ShapeDtypeStruct((T, D_MODEL), F32),
        compiler_params=_cp("parallel"),
        name="merge",
    )(y_ret, y_rwkv, y_mla, p, p, p, h, w_branch.astype(BF16), w_o.astype(BF16))


def _rms(x, g):
    return x * lax.rsqrt(jnp.mean(x * x, axis=-1, keepdims=True) + NORM_EPS) * g


FFN_TF = 256


def _ffn_kernel(h_ref, g_ref, wg_ref, wu_ref, wd_ref, o_ref):
    h = h_ref[...]
    hn = _rms(h, g_ref[...]).astype(BF16)
    acc = h
    for c in range(D_FF // FFN_TF):
        sl = slice(c * FFN_TF, (c + 1) * FFN_TF)
        a = _dot(hn, wg_ref[:, sl])
        u = _dot(hn, wu_ref[:, sl])
        acc = acc + _dot((a * _sigmoid(a) * u).astype(BF16), wd_ref[sl, :])
    o_ref[...] = acc


def dense_ffn(h, g, wg, wu, wd):
    T = h.shape[0]
    tm = 512
    full = lambda shape: pl.BlockSpec(shape, lambda i: (0, 0))
    return pl.pallas_call(
        _ffn_kernel,
        grid=(T // tm,),
        in_specs=[pl.BlockSpec((tm, D_MODEL), lambda i: (i, 0)), full((1, D_MODEL)),
                  full((D_MODEL, D_FF)), full((D_MODEL, D_FF)), full((D_FF, D_MODEL))],
        out_specs=pl.BlockSpec((tm, D_MODEL), lambda i: (i, 0)),
        out_shape=jax.ShapeDtypeStruct((T, D_MODEL), F32),
        compiler_params=_cp("parallel"),
        name="dense_ffn",
    )(h, g[None, :], wg.astype(BF16), wu.astype(BF16), wd.astype(BF16))


def _router_kernel(h_ref, g_ref, rhi_ref, rlo_ref, o_ref):
    hn = _rms(h_ref[...], g_ref[...])
    hi, mid, _ = _split3(hn)
    logits = _dot(hi, rhi_ref[...]) + (_dot(hi, rlo_ref[...]) + _dot(mid, rhi_ref[...]))
    lane = lax.broadcasted_iota(jnp.int32, logits.shape, 1)
    neg = jnp.float32(-1e30)
    logits = jnp.where(lane < N_EXPERTS, logits, neg)
    m1 = jnp.max(logits, axis=-1, keepdims=True)
    i1 = jnp.min(jnp.where(logits == m1, lane, 128), axis=-1, keepdims=True)
    rest = jnp.where(lane == i1, neg, logits)
    m2 = jnp.max(rest, axis=-1, keepdims=True)
    i2 = jnp.min(jnp.where(rest == m2, lane, 128), axis=-1, keepdims=True)
    e2 = jnp.exp(m2 - m1)
    w1 = 1.0 / (1.0 + e2)
    w2 = e2 / (1.0 + e2)
    o_ref[...] = (jnp.where(lane == i1, w1, 0.0) + jnp.where(lane == i2, w2, 0.0)
                  + jnp.where(lane == 8, w1, 0.0) + jnp.where(lane == 9, w2, 0.0)
                  + jnp.where(lane == 10, i1.astype(F32), 0.0) + jnp.where(lane == 11, i2.astype(F32), 0.0))


def router(h, g, w_router):
    T = h.shape[0]
    tm = 512
    wr = jnp.pad(w_router.astype(F32), ((0, 0), (0, 128 - N_EXPERTS)))
    rhi = wr.astype(BF16)
    rlo = (wr - rhi.astype(F32)).astype(BF16)
    full = lambda shape: pl.BlockSpec(shape, lambda i: (0, 0))
    return pl.pallas_call(
        _router_kernel,
        grid=(T // tm,),
        in_specs=[pl.BlockSpec((tm, D_MODEL), lambda i: (i, 0)), full((1, D_MODEL)),
                  full((D_MODEL, 128)), full((D_MODEL, 128))],
        out_specs=pl.BlockSpec((tm, 128), lambda i: (i, 0)),
        out_shape=jax.ShapeDtypeStruct((T, 128), F32),
        compiler_params=_cp("parallel"),
        name="router",
    )(h, g[None, :], rhi, rlo)


MOE_TM = 512
MOE_TF = 1792
MOE_DISPATCH_TM = 512


def _row_copy(src_ref, src_row, dst_ref, dst_row, sem):
    return pltpu.make_async_copy(src_ref.at[pl.ds(src_row, 1)], dst_ref.at[pl.ds(dst_row, 1)], sem)


def _dispatch_kernel(pos1_ref, pos2_ref, ends_ref, h_ref, g_ref, xs_ref, hn_ref, zero_ref, sems, zsem):
    i = pl.program_id(0)
    n = pl.num_programs(0)
    tm = h_ref.shape[0]

    @pl.when(i == 0)
    def _():
        zero_ref[...] = jnp.zeros_like(zero_ref)
        for wait in (False, True):
            for e in range(N_EXPERTS):
                lo = ends_ref[e - 1] if e else 0

                @pl.when(ends_ref[e] > lo)
                def _():
                    start = pl.multiple_of(ends_ref[e] - MOE_TM, MOE_TM)
                    cp = pltpu.make_async_copy(zero_ref, xs_ref.at[pl.ds(start, MOE_TM)], zsem)
                    cp.wait() if wait else cp.start()

            def unused_tile(t, c, wait=wait):
                cp = pltpu.make_async_copy(zero_ref, xs_ref.at[pl.ds(pl.multiple_of(t * MOE_TM, MOE_TM), MOE_TM)], zsem)
                cp.wait() if wait else cp.start()
                return c

            lax.fori_loop(ends_ref[N_EXPERTS - 1] // MOE_TM, xs_ref.shape[0] // MOE_TM, unused_tile, 0)

    def rows(tile, slot, wait):
        base = tile * tm

        def one(r, c):
            for pos_ref in (pos1_ref, pos2_ref):
                cp = _row_copy(hn_ref.at[slot], r, xs_ref, pos_ref[base + r], sems.at[slot])
                cp.wait() if wait else cp.start()
            return c

        lax.fori_loop(0, tm, one, 0, unroll=8)

    slot = lax.rem(i, 2)

    @pl.when(i >= 2)
    def _():
        rows(i - 2, slot, True)

    hn_ref[slot] = _rms(h_ref[...], g_ref[...])
    rows(i, slot, False)

    @pl.when(i == n - 1)
    def _():
        @pl.when(i >= 1)
        def _():
            rows(i - 1, 1 - slot, True)

        rows(i, slot, True)


def _moe_kernel(te_ref, nv_ref, xs_ref, wg_ref, wu_ref, wd_ref, y_ref, xb_ref, acc_ref):
    del te_ref
    i = pl.program_id(0)
    f = pl.program_id(1)

    @pl.when(i < nv_ref[0])
    def _():
        @pl.when(f == 0)
        def _():
            xb_ref[...] = xs_ref[...].astype(BF16)

        xb = xb_ref[...]
        a = _dot(xb, wg_ref[0])
        u = _dot(xb, wu_ref[0])
        t = _dot((a * _sigmoid(a) * u).astype(BF16), wd_ref[0])

        @pl.when(f == 0)
        def _():
            acc_ref[...] = t

        @pl.when(f > 0)
        def _():
            acc_ref[...] += t

        @pl.when(f == pl.num_programs(1) - 1)
        def _():
            y_ref[...] = acc_ref[...]

    @pl.when(i >= nv_ref[0])
    def _():
        y_ref[...] = jnp.zeros_like(y_ref)


def _combine_kernel(pos1_ref, pos2_ref, h_ref, rt_ref, fin_ref, y_ref, o_ref, ybuf_ref, sems):
    i = pl.program_id(0)
    n = pl.num_programs(0)
    tm = h_ref.shape[0]

    def rows(tile, slot, wait):
        base = tile * tm

        def one(r, c):
            for k, pos_ref in enumerate((pos1_ref, pos2_ref)):
                cp = _row_copy(y_ref, pos_ref[base + r], ybuf_ref.at[slot, k], r, sems.at[slot])
                cp.wait() if wait else cp.start()
            return c

        lax.fori_loop(0, tm, one, 0, unroll=8)

    slot = lax.rem(i, 2)

    @pl.when(i == 0)
    def _():
        rows(i, slot, False)

    @pl.when(i + 1 < n)
    def _():
        rows(i + 1, 1 - slot, False)

    rows(i, slot, True)
    rt = rt_ref[...]
    lane = lax.broadcasted_iota(jnp.int32, rt.shape, 1)
    w1 = jnp.sum(jnp.where(lane == 8, rt, 0.0), axis=-1, keepdims=True)
    w2 = jnp.sum(jnp.where(lane == 9, rt, 0.0), axis=-1, keepdims=True)
    o_ref[...] = _rms(h_ref[...] + (w1 * ybuf_ref[slot, 0] + w2 * ybuf_ref[slot, 1]), fin_ref[...])


def _moe_plan(rt, n_tiles):
    e1 = rt[:, 10].astype(jnp.int32)
    e2 = rt[:, 11].astype(jnp.int32)
    eid = jnp.arange(N_EXPERTS, dtype=jnp.int32)[None, :]
    m1 = (e1[:, None] == eid).astype(jnp.int32)
    m2 = (e2[:, None] == eid).astype(jnp.int32)
    m = m1 + m2
    counts = jnp.sum(m, axis=0)
    rank = jnp.cumsum(m, axis=0) - m
    padded = ((counts + MOE_TM - 1) // MOE_TM) * MOE_TM
    ends = jnp.cumsum(padded)
    pos_te = (ends - padded)[None, :] + rank
    pos1 = jnp.sum(m1 * pos_te, axis=1).astype(jnp.int32)
    pos2 = jnp.sum(m2 * pos_te, axis=1).astype(jnp.int32)
    nvalid = (ends[-1] // MOE_TM).astype(jnp.int32)
    starts = jnp.minimum(jnp.arange(n_tiles, dtype=jnp.int32), nvalid - 1) * MOE_TM
    tile_expert = jnp.sum((starts[:, None] >= ends[None, :]).astype(jnp.int32), axis=1)
    return pos1, pos2, ends.astype(jnp.int32), tile_expert.astype(jnp.int32), nvalid.reshape(1)


def moe_final(h, g, rt, wg, wu, wd, final_g):
    T = h.shape[0]
    n_tiles = (2 * T) // MOE_TM + N_EXPERTS
    n_rows = n_tiles * MOE_TM
    nf = D_FF_EXPERT // MOE_TF
    pos1, pos2, ends, tile_expert, nvalid = _moe_plan(rt, n_tiles)

    tm = MOE_DISPATCH_TM
    xs = pl.pallas_call(
        _dispatch_kernel,
        grid_spec=pltpu.PrefetchScalarGridSpec(
            num_scalar_prefetch=3,
            grid=(T // tm,),
            in_specs=[pl.BlockSpec((tm, D_MODEL), lambda i, p1, p2, en: (i, 0)),
                      pl.BlockSpec((1, D_MODEL), lambda i, p1, p2, en: (0, 0))],
            out_specs=pl.BlockSpec(memory_space=pl.ANY),
            scratch_shapes=[pltpu.VMEM((2, tm, D_MODEL), F32), pltpu.VMEM((MOE_TM, D_MODEL), F32),
                            pltpu.SemaphoreType.DMA((2,)), pltpu.SemaphoreType.DMA(())]),
        out_shape=jax.ShapeDtypeStruct((n_rows, D_MODEL), F32),
        compiler_params=_cp("arbitrary"),
        name="moe_dispatch",
    )(pos1, pos2, ends, h, g[None, :])

    def row_tile(i, f, te, nv):
        return (jnp.minimum(i, nv[0] - 1), 0)

    def w_up(i, f, te, nv):
        return (te[i], 0, jnp.where(i < nv[0], f, nf - 1))

    def w_down(i, f, te, nv):
        return (te[i], jnp.where(i < nv[0], f, nf - 1), 0)

    y = pl.pallas_call(
        _moe_kernel,
        grid_spec=pltpu.PrefetchScalarGridSpec(
            num_scalar_prefetch=2,
            grid=(n_tiles, nf),
            in_specs=[pl.BlockSpec((MOE_TM, D_MODEL), row_tile),
                      pl.BlockSpec((1, D_MODEL, MOE_TF), w_up),
                      pl.BlockSpec((1, D_MODEL, MOE_TF), w_up),
                      pl.BlockSpec((1, MOE_TF, D_MODEL), w_down)],
            out_specs=pl.BlockSpec((MOE_TM, D_MODEL), lambda i, f, te, nv: (i, 0)),
            scratch_shapes=[pltpu.VMEM((MOE_TM, D_MODEL), BF16), pltpu.VMEM((MOE_TM, D_MODEL), F32)]),
        out_shape=jax.ShapeDtypeStruct((n_rows, D_MODEL), F32),
        compiler_params=_cp("arbitrary", "arbitrary"),
        name="moe_experts",
    )(tile_expert, nvalid, xs, wg.astype(BF16), wu.astype(BF16), wd.astype(BF16))

    return pl.pallas_call(
        _combine_kernel,
        grid_spec=pltpu.PrefetchScalarGridSpec(
            num_scalar_prefetch=2,
            grid=(T // tm,),
            in_specs=[pl.BlockSpec((tm, D_MODEL), lambda i, p1, p2: (i, 0)),
                      pl.BlockSpec((tm, 128), lambda i, p1, p2: (i, 0)),
                      pl.BlockSpec((1, D_MODEL), lambda i, p1, p2: (0, 0)),
                      pl.BlockSpec(memory_space=pl.ANY)],
            out_specs=pl.BlockSpec((tm, D_MODEL), lambda i, p1, p2: (i, 0)),
            scratch_shapes=[pltpu.VMEM((2, 2, tm, D_MODEL), F32), pltpu.SemaphoreType.DMA((2,))]),
        out_shape=jax.ShapeDtypeStruct((T, D_MODEL), F32),
        compiler_params=_cp("arbitrary"),
        name="moe_combine",
    )(pos1, pos2, h, rt, final_g[None, :], y)


def _final_norm_kernel(h_ref, g_ref, o_ref):
    o_ref[...] = _rms(h_ref[...], g_ref[...])


def final_norm(h, g):
    T = h.shape[0]
    tm = 1024
    return pl.pallas_call(
        _final_norm_kernel,
        grid=(T // tm,),
        in_specs=[pl.BlockSpec((tm, D_MODEL), lambda i: (i, 0)), pl.BlockSpec((1, D_MODEL), lambda i: (0, 0))],
        out_specs=pl.BlockSpec((tm, D_MODEL), lambda i: (i, 0)),
        out_shape=jax.ShapeDtypeStruct((T, D_MODEL), F32),
        compiler_params=_cp("parallel"),
        name="final_norm",
    )(h, g[None, :])


def kernel(x, positions, attn_norm, w_in, rwkv_mu, rwkv_w0, rwkv_w2, rwkv_a0, rwkv_a2, rwkv_g2, rwkv_k_k, rwkv_k_a,
           rwkv_r_k, rwkv_lnx_w, mla_q_norm, mla_w_qb, mla_kv_norm, mla_w_kvb, w_branch, w_o, ffn_norm, ffn_w_gate,
           ffn_w_up, ffn_w_down, moe_router, moe_w_gate, moe_w_up, moe_w_down, final_norm_g):
    B, S, D = x.shape
    T = B * S
    depth = attn_norm.shape[0]
    rwkv_all = dict(rwkv_mu=rwkv_mu, rwkv_w0=rwkv_w0, rwkv_w2=rwkv_w2, rwkv_a0=rwkv_a0, rwkv_a2=rwkv_a2,
                    rwkv_g2=rwkv_g2, rwkv_k_k=rwkv_k_k, rwkv_k_a=rwkv_k_a, rwkv_r_k=rwkv_r_k,
                    rwkv_lnx_w=rwkv_lnx_w)
    tab = rope_tables(positions)
    h = x.reshape(T, D)
    out = None
    for l in range(depth):
        p = norm_in_proj(h, attn_norm[l][None, :], relayout_w_in(w_in[l]))
        y_ret = retention(p, tab, B, S)
        y_rwkv = rwkv7(p, {k: v[l] for k, v in rwkv_all.items()}, B, S)
        y_mla = mla(p, tab, mla_q_norm[l], mla_w_qb[l], mla_kv_norm[l], mla_w_kvb[l], B, S)
        h = merge(y_ret, y_rwkv, y_mla, p, h, w_branch[l], w_o[l])
        last = l == depth - 1
        if l % 2 == 0:
            h = dense_ffn(h, ffn_norm[l], ffn_w_gate[l // 2], ffn_w_up[l // 2], ffn_w_down[l // 2])
            if last:
                out = final_norm(h, final_norm_g)
        else:
            rt = router(h, ffn_norm[l], moe_router[l // 2])
            out_l = moe_final(h, ffn_norm[l], rt, moe_w_gate[l // 2], moe_w_up[l // 2], moe_w_down[l // 2],
                              final_norm_g)
            if last:
                out = out_l
            else:
                raise NotImplementedError("a MoE layer that is not the last layer")
    return out.reshape(B, S, D)
```

```python
import functools

import numpy as np
import jax
import jax.numpy as jnp
from jax import lax
from jax.experimental import pallas as pl
from jax.experimental.pallas import tpu as pltpu

F32 = jnp.float32
BF16 = jnp.bfloat16

D_MODEL = 1024
CHUNK = 64
NORM_EPS = 1e-6
ROPE_BASE = 10000.0

RET_HEADS = 4
RET_DK = 128
RET_GN_EPS = 1e-5

RWKV_HEAD = 64
RWKV_HEADS = 8
RWKV_W = 512
RWKV_GN_EPS = 64e-5
RWKV_COLS = 1824

MLA_HEADS = 8
MLA_Q_LORA = 256
MLA_KV_LORA = 128
MLA_NOPE = 64
MLA_ROPE = 32
MLA_V = 64

N_EXPERTS = 8
D_FF = 2816
D_FF_EXPERT = 3584

P_RET = 0
P_RWKV = 2048
P_GATE = 4096
P_MLA = 7168
P_COLS = 7680

BLK = 256
VMEM_LIMIT_BYTES = 56 * 1024 * 1024


def _cp(*sem):
    return pltpu.CompilerParams(dimension_semantics=sem, vmem_limit_bytes=VMEM_LIMIT_BYTES)


def _sigmoid(z):
    return 1.0 / (1.0 + jnp.exp(-z))


def _split3(x):
    hi = x.astype(BF16)
    r1 = x - hi.astype(F32)
    mid = r1.astype(BF16)
    lo = (r1 - mid.astype(F32)).astype(BF16)
    return hi, mid, lo


def _dot(a, b):
    return jnp.dot(a, b, preferred_element_type=F32)


def _dot_nt(a, b):
    return lax.dot_general(a, b, (((1,), (1,)), ((), ())), preferred_element_type=F32)


def _dot_tn(a, b):
    return lax.dot_general(a, b, (((0,), (0,)), ((), ())), preferred_element_type=F32)


def _dot_exact_lhs(m_bf16, x_f32):
    hi, mid, lo = _split3(x_f32)
    return _dot(m_bf16, hi) + _dot(m_bf16, mid) + _dot(m_bf16, lo)


def _dot_exact_rhs(x_f32, m_bf16):
    hi, mid, lo = _split3(x_f32)
    return _dot(hi, m_bf16) + _dot(mid, m_bf16) + _dot(lo, m_bf16)


def _dot_split2_rhs(x_f32, m_bf16):
    hi, mid, _ = _split3(x_f32)
    return _dot(hi, m_bf16) + _dot(mid, m_bf16)


TAB_COLS = 768


def _tables_kernel(pos_ref, freq_ref, perm_ref, o_ref):
    ang = pos_ref[...] * freq_ref[...]
    cs = jnp.concatenate([jnp.cos(ang), jnp.sin(ang)], axis=1)
    o_ref[...] = _dot_exact_rhs(cs, perm_ref[...])


def _table_perm():
    p = np.zeros((256, TAB_COLS), np.float32)
    c, s = 0, 128
    for j in range(64):
        p[c + j, j] = 1.0
        p[c + j, 64 + j] = 1.0
        p[s + j, 128 + j] = -1.0
        p[s + j, 128 + 64 + j] = 1.0
    for j in range(64):
        p[c + 127, 256 + j] = 1.0
    for j in range(16):
        cj, sj = c + 64 + j, s + 64 + j
        q0 = 256 + 64
        p[cj, q0 + j] = 1.0
        p[sj, q0 + 16 + j] = 1.0
        p[sj, q0 + 32 + j] = 1.0
        p[cj, q0 + 48 + j] = 1.0
        k0 = 384
        p[cj, k0 + j] = 1.0
        p[cj, k0 + 16 + j] = -1.0
        p[sj, k0 + 32 + j] = 1.0
        p[sj, k0 + 48 + j] = 1.0
        p[sj, k0 + 64 + j] = -1.0
        p[sj, k0 + 80 + j] = 1.0
        p[cj, k0 + 96 + j] = 1.0
        p[cj, k0 + 112 + j] = 1.0
    return p


def rope_tables(positions):
    T = positions.size
    tm = 512
    inv_ret = 1.0 / (ROPE_BASE ** (jnp.arange(0, RET_DK, 2, dtype=F32) / RET_DK))
    inv_mla = 1.0 / (ROPE_BASE ** (jnp.arange(0, MLA_ROPE, 2, dtype=F32) / MLA_ROPE))
    freq = jnp.concatenate([inv_ret, inv_mla, jnp.zeros((48,), F32)])[None, :]
    pos = positions.reshape(T, 1).astype(F32)
    perm = jnp.asarray(_table_perm(), BF16)
    return pl.pallas_call(
        _tables_kernel,
        grid=(T // tm,),
        in_specs=[pl.BlockSpec((tm, 1), lambda i: (i, 0)),
                  pl.BlockSpec((1, 128), lambda i: (0, 0)),
                  pl.BlockSpec((256, TAB_COLS), lambda i: (0, 0))],
        out_specs=pl.BlockSpec((tm, TAB_COLS), lambda i: (i, 0)),
        out_shape=jax.ShapeDtypeStruct((T, TAB_COLS), F32),
        compiler_params=_cp("parallel"),
        name="rope_tables",
    )(pos, freq, perm)


def _norm_matmul_kernel(x_ref, g_ref, w_ref, o_ref, hn_ref):
    @pl.when(pl.program_id(1) == 0)
    def _():
        x = x_ref[...]
        ms = jnp.mean(x * x, axis=-1, keepdims=True)
        hn_ref[...] = (x * lax.rsqrt(ms + NORM_EPS) * g_ref[...]).astype(BF16)

    o_ref[...] = _dot(hn_ref[...], w_ref[...]).astype(o_ref.dtype)


def norm_in_proj(h, g, w):
    T = h.shape[0]
    n = w.shape[1]
    tm, tn = min(2048, T), 1536
    return pl.pallas_call(
        _norm_matmul_kernel,
        grid=(T // tm, n // tn),
        in_specs=[pl.BlockSpec((tm, D_MODEL), lambda i, j: (i, 0)),
                  pl.BlockSpec((1, D_MODEL), lambda i, j: (0, 0)),
                  pl.BlockSpec((D_MODEL, tn), lambda i, j: (0, j))],
        out_specs=pl.BlockSpec((tm, tn), lambda i, j: (i, j)),
        out_shape=jax.ShapeDtypeStruct((T, n), BF16),
        scratch_shapes=[pltpu.VMEM((tm, D_MODEL), BF16)],
        compiler_params=_cp("parallel", "arbitrary"),
        name="norm_in_proj",
    )(h, g, w)


def relayout_w_in(w):
    d = w.shape[0]
    kpe = 3872 + 256 + 128
    return jnp.concatenate([
        w[:, :3872],
        jnp.zeros((d, P_GATE - 3872), w.dtype),
        w[:, 4288:7360],
        w[:, 3872:4256],
        jnp.tile(w[:, kpe:kpe + 16], (1, 4)),
        jnp.tile(w[:, kpe + 16:kpe + 32], (1, 4)),
    ], axis=1).astype(BF16)


def _retention_kernel(q_ref, k_ref, v_ref, g_ref, cos_ref, sin_ref, mask_ref, qd_ref, kd_ref, cd_ref,
                      o_ref, st_ref):
    @pl.when(pl.program_id(1) == 0)
    def _():
        st_ref[...] = jnp.zeros_like(st_ref)

    cos2 = cos_ref[...]
    sin2 = sin_ref[...]
    for h in range(RET_HEADS):
        sl = slice(h * RET_DK, (h + 1) * RET_DK)
        q = q_ref[:, sl].astype(F32)
        k = k_ref[:, sl].astype(F32)
        q = q * cos2 + pltpu.roll(q, 64, 1) * sin2
        k = (k * cos2 + pltpu.roll(k, 64, 1) * sin2) * (RET_DK ** -0.5)
        v = v_ref[:, sl]
        qb = q.astype(BF16)
        kb = k.astype(BF16)
        scores = _dot_nt(qb, kb) * mask_ref[h]
        o = _dot(scores.astype(BF16), v)
        st = st_ref[h]
        o = o + _dot((q * qd_ref[h]).astype(BF16), st.astype(BF16))
        st_ref[h] = st * cd_ref[h] + _dot_tn((k * kd_ref[h]).astype(BF16), v)
        mean = jnp.mean(o, axis=-1, keepdims=True)
        oc = o - mean
        var = jnp.mean(oc * oc, axis=-1, keepdims=True)
        y = oc * lax.rsqrt(var + RET_GN_EPS)
        g = g_ref[:, sl].astype(F32)
        o_ref[:, sl] = (y * (g * _sigmoid(g))).astype(o_ref.dtype)


def _retention_consts():
    hh = jnp.arange(RET_HEADS, dtype=F32)
    log_gamma = jnp.log(1.0 - 2.0 ** (-5.0 - hh))
    idx = jnp.arange(BLK, dtype=F32)
    dist = jnp.abs(idx[:, None] - idx[None, :])
    ci = np.arange(BLK) // CHUNK
    visible = jnp.asarray(ci[None, :] <= ci[:, None])
    mask = jnp.where(visible[None], jnp.exp(log_gamma[:, None, None] * dist[None]), 0.0)
    qd = jnp.exp(log_gamma[:, None] * (idx[None, :] + 1.0))
    kd = jnp.exp(log_gamma[:, None] * (BLK - 1.0 - idx[None, :]))
    cd = jnp.exp(log_gamma * BLK)
    qd = jnp.broadcast_to(qd[:, :, None], (RET_HEADS, BLK, RET_DK))
    kd = jnp.broadcast_to(kd[:, :, None], (RET_HEADS, BLK, RET_DK))
    cd = jnp.broadcast_to(cd[:, None, None], (RET_HEADS, 1, RET_DK))
    return mask, qd, kd, cd


def retention(p, tab, B, S):
    T = B * S
    nb = S // BLK
    mask, qd, kd, cd = _retention_consts()
    row = lambda c: pl.BlockSpec((BLK, 512), lambda b, i, c=c: (b * nb + i, c))
    tcol = lambda c: pl.BlockSpec((BLK, 128), lambda b, i, c=c: (b * nb + i, c))
    const3 = lambda shape: pl.BlockSpec(shape, lambda b, i: (0, 0, 0))
    return pl.pallas_call(
        _retention_kernel,
        grid=(B, nb),
        in_specs=[row(0), row(1), row(2), row(3), tcol(0), tcol(1),
                  const3((RET_HEADS, BLK, BLK)), const3((RET_HEADS, BLK, RET_DK)),
                  const3((RET_HEADS, BLK, RET_DK)), const3((RET_HEADS, 1, RET_DK))],
        out_specs=pl.BlockSpec((BLK, 512), lambda b, i: (b * nb + i, 0)),
        out_shape=jax.ShapeDtypeStruct((T, 512), BF16),
        scratch_shapes=[pltpu.VMEM((RET_HEADS, RET_DK, RET_DK), F32)],
        compiler_params=_cp("arbitrary", "arbitrary"),
        name="retention",
    )(p, p, p, p, tab, tab, mask, qd, kd, cd)


def _rwkv_pre_kernel(p_ref, mu_ref, w0_ref, w2_ref, a0_ref, a2_ref, g2_ref, kk_w_ref, ka_ref, rk_ref, seg_ref,
                     r_o, w_o, k_o, v_o, kk_o, b_o, bonus_o, g_o, prev_ref):
    tm = p_ref.shape[0]

    @pl.when(pl.program_id(1) == 0)
    def _():
        prev_ref[...] = jnp.zeros_like(prev_ref)

    x = p_ref[...].astype(F32)
    row = lax.broadcasted_iota(jnp.int32, x.shape, 0)
    shifted = jnp.where(row == 0, prev_ref[7:8, :], pltpu.roll(x, 1, 0))
    prev_ref[...] = x[tm - 8:tm, :]
    xs = x + (shifted - x) * mu_ref[...]
    r = xs[:, 0:512]
    k = xs[:, 512:1024]
    v = xs[:, 1024:1536]
    xwa = xs[:, 1536:1664]
    xg = xs[:, 1664:1920]
    z = w0_ref[...] + _dot(jnp.tanh(xwa).astype(BF16), w2_ref[...])
    logw = -_sigmoid(z) * float(np.exp(-0.5))
    a = _sigmoid(a0_ref[...] + _dot(xwa.astype(BF16), a2_ref[...]))
    g = _dot(_sigmoid(xg).astype(BF16), g2_ref[...])
    seg = seg_ref[...]
    kk = k * kk_w_ref[...]
    kk = kk * lax.rsqrt(_dot_split2_rhs(kk * kk, seg) + 1e-12)
    k = k * (1.0 + (a - 1.0) * ka_ref[...])
    bonus = _dot_split2_rhs(r * k * rk_ref[...], seg) * v
    r_o[...] = r.astype(r_o.dtype)
    w_o[...] = logw
    k_o[...] = k.astype(k_o.dtype)
    v_o[...] = v.astype(v_o.dtype)
    kk_o[...] = kk.astype(kk_o.dtype)
    b_o[...] = (kk * a).astype(b_o.dtype)
    bonus_o[...] = bonus.astype(bonus_o.dtype)
    g_o[...] = g.astype(g_o.dtype)


RWKV_PAIRS_PER_STEP = 4


def _rwkv_rec_kernel(r_ref, w_ref, k_ref, v_ref, kk_ref, b_ref, bonus_ref, g_ref, lnx_ref, o_ref, st_ref, y_ref):
    @pl.when(pl.program_id(2) == 0)
    def _():
        st_ref[...] = jnp.zeros_like(st_ref)

    row = lax.broadcasted_iota(jnp.int32, (BLK, BLK), 0)
    col = lax.broadcasted_iota(jnp.int32, (BLK, BLK), 1)
    same = (row >> 6) == (col >> 6)
    incl = same & (col <= row)
    strict = same & (col < row)
    tri = jnp.where(incl, 1.0, 0.0).astype(BF16)
    lane = lax.broadcasted_iota(jnp.int32, (BLK, 128), 1)
    head0 = lane < RWKV_HEAD
    rk_ = lax.broadcasted_iota(jnp.int32, (128, 128), 0)
    ck_ = lax.broadcasted_iota(jnp.int32, (128, 128), 1)
    bd = (rk_ >> 6) == (ck_ >> 6)
    n_chunks = BLK // CHUNK
    pairs = range(RWKV_PAIRS_PER_STEP)
    lanes = [slice(pi * 128, (pi + 1) * 128) for pi in pairs]

    vb, rt, kkt, yt, bh, kh, wc = [], [], [], [], [], [], []
    cum_all = _dot_exact_lhs(tri, w_ref[...])
    for ps in lanes:
        r = r_ref[:, ps].astype(F32)
        k = k_ref[:, ps].astype(F32)
        kk = kk_ref[:, ps].astype(F32)
        b = b_ref[:, ps].astype(F32)
        logw = w_ref[:, ps]
        cum = cum_all[:, ps]
        tot = jnp.concatenate([jnp.broadcast_to(cum[(c + 1) * CHUNK - 1:(c + 1) * CHUNK, :], (CHUNK, 128))
                               for c in range(n_chunks)], axis=0)
        e_neg = jnp.exp(-cum)
        e_end = jnp.exp(tot - cum)
        vb.append(v_ref[:, ps])
        rt.append(r * jnp.exp(cum))
        kkt.append(kk * jnp.exp(cum - logw))
        yt.append(jnp.concatenate([b * e_neg, k * e_neg], axis=0).astype(BF16))
        bh.append(b * e_end)
        kh.append(k * e_end)
        wc.append(jnp.exp(tot))

    streams = [(pi, h) for pi in pairs for h in range(2)]
    apow, xs, lrb, lrk = [], [], [], []
    for pi, h in streams:
        mh = head0 if h == 0 else jnp.logical_not(head0)
        xh = jnp.concatenate([jnp.where(mh, kkt[pi], 0.0), jnp.where(mh, rt[pi], 0.0)], axis=0).astype(BF16)
        gm = _dot_nt(xh, yt[pi])
        apow.append(jnp.where(strict, gm[:BLK, :BLK], 0.0))
        a_uk = jnp.where(strict, gm[:BLK, BLK:], 0.0)
        lrb.append(jnp.where(incl, gm[BLK:, :BLK], 0.0).astype(BF16))
        lrk.append(jnp.where(incl, gm[BLK:, BLK:], 0.0).astype(BF16))
        xs.append(jnp.concatenate([kkt[pi], _dot(a_uk.astype(BF16), vb[pi])], axis=1))
    xs = [x - _dot(a.astype(BF16), x.astype(BF16)) for a, x in zip(apow, xs)]
    for _ in range(5):
        apow = [_dot(a.astype(BF16), a.astype(BF16)) for a in apow]
        xs = [x + _dot(a.astype(BF16), x.astype(BF16)) for a, x in zip(apow, xs)]

    pm, qm, rp, y0 = [], [], [], []
    for pi in pairs:
        x0, x1 = xs[2 * pi], xs[2 * pi + 1]
        pm.append(jnp.where(head0, x0[:, :128], x1[:, :128]))
        qm.append(jnp.where(head0, x0[:, 128:], x1[:, 128:]))
    for pi in pairs:
        pq = jnp.concatenate([pm[pi], qm[pi]], axis=1).astype(BF16)
        lp0 = _dot(lrb[2 * pi], pq)
        lp1 = _dot(lrb[2 * pi + 1], pq)
        rp.append(rt[pi] - jnp.where(head0, lp0[:, :128], lp1[:, :128]))
        y0.append(jnp.where(head0, _dot(lrk[2 * pi], vb[pi]) - lp0[:, 128:],
                            _dot(lrk[2 * pi + 1], vb[pi]) - lp1[:, 128:]))

    ptb = [[None] * n_chunks for _ in pairs]
    nn = [[None] * n_chunks for _ in pairs]
    for c in range(n_chunks):
        sl = slice(c * CHUNK, (c + 1) * CHUNK)
        for pi in pairs:
            bhc = bh[pi][sl].astype(BF16)
            ptb[pi][c] = jnp.where(bd, _dot_tn(pm[pi][sl].astype(BF16), bhc), 0.0).astype(BF16)
            lhs = jnp.concatenate([vb[pi][sl], (-qm[pi][sl]).astype(BF16)], axis=0)
            rhs = jnp.concatenate([kh[pi][sl].astype(BF16), bhc], axis=0)
            nn[pi][c] = jnp.where(bd, _dot_tn(lhs, rhs), 0.0)

    st = [st_ref[pi] for pi in pairs]
    for c in range(n_chunks):
        sl = slice(c * CHUNK, (c + 1) * CHUNK)
        for pi in pairs:
            stb = st[pi].astype(BF16)
            y_ref[sl, lanes[pi]] = _dot_nt(rp[pi][sl].astype(BF16), stb) + y0[pi][sl]
            st[pi] = st[pi] * wc[pi][c * CHUNK:c * CHUNK + 1, :] - _dot(stb, ptb[pi][c]) + nn[pi][c]
    for pi in pairs:
        st_ref[pi] = st[pi]

    def head_mean(x):
        s0 = jnp.sum(jnp.where(head0, x, 0.0), axis=-1, keepdims=True)
        s1 = jnp.sum(x, axis=-1, keepdims=True) - s0
        return jnp.where(head0, s0, s1) * (1.0 / RWKV_HEAD)

    for ps in lanes:
        y = y_ref[:, ps]
        yc = y - head_mean(y)
        var = head_mean(yc * yc)
        yn = yc * lax.rsqrt(var + RWKV_GN_EPS) * lnx_ref[:, ps]
        o_ref[:, ps] = ((yn + bonus_ref[:, ps].astype(F32)) * g_ref[:, ps].astype(F32)).astype(o_ref.dtype)


def rwkv7(p, prm, B, S):
    T = B * S
    tm = 512
    nb = S // tm
    pad1 = lambda a, n: jnp.pad(a.astype(F32), (0, n - a.shape[0]))[None, :]
    mu = pad1(prm['rwkv_mu'], 2048)
    w2 = jnp.zeros((128, RWKV_W), F32).at[0:64].set(prm['rwkv_w2']).astype(BF16)
    a2 = jnp.zeros((128, RWKV_W), F32).at[64:128].set(prm['rwkv_a2']).astype(BF16)
    g2 = jnp.zeros((256, RWKV_W), F32).at[0:160].set(prm['rwkv_g2']).astype(BF16)
    hid = np.arange(RWKV_W) // RWKV_HEAD
    seg = jnp.asarray(hid[:, None] == hid[None, :], BF16)
    row1 = lambda a: a.reshape(1, RWKV_W).astype(F32)
    vec = pl.BlockSpec((1, RWKV_W), lambda b, i: (0, 0))
    out_blk = pl.BlockSpec((tm, RWKV_W), lambda b, i: (b * nb + i, 0))
    sds = lambda dt: jax.ShapeDtypeStruct((T, RWKV_W), dt)
    r, logw, k, v, kk, bb, bonus, g = pl.pallas_call(
        _rwkv_pre_kernel,
        grid=(B, nb),
        in_specs=[pl.BlockSpec((tm, 2048), lambda b, i: (b * nb + i, 1)),
                  pl.BlockSpec((1, 2048), lambda b, i: (0, 0)),
                  vec, pl.BlockSpec((128, RWKV_W), lambda b, i: (0, 0)),
                  vec, pl.BlockSpec((128, RWKV_W), lambda b, i: (0, 0)),
                  pl.BlockSpec((256, RWKV_W), lambda b, i: (0, 0)),
                  vec, vec, vec,
                  pl.BlockSpec((RWKV_W, RWKV_W), lambda b, i: (0, 0))],
        out_specs=[out_blk] * 8,
        out_shape=[sds(BF16), sds(F32), sds(BF16), sds(BF16), sds(BF16), sds(BF16), sds(BF16), sds(BF16)],
        scratch_shapes=[pltpu.VMEM((8, 2048), F32)],
        compiler_params=_cp("arbitrary", "arbitrary"),
        name="rwkv_pre",
    )(p, mu, row1(prm['rwkv_w0']), w2, row1(prm['rwkv_a0']), a2, g2,
      row1(prm['rwkv_k_k']), row1(prm['rwkv_k_a']), row1(prm['rwkv_r_k']), seg)

    nblk = S // BLK
    lanes = 128 * RWKV_PAIRS_PER_STEP
    blk = pl.BlockSpec((BLK, lanes), lambda b, j, i: (b * nblk + i, j))
    return pl.pallas_call(
        _rwkv_rec_kernel,
        grid=(B, RWKV_W // lanes, nblk),
        in_specs=[blk] * 8 + [pl.BlockSpec((1, lanes), lambda b, j, i: (0, j))],
        out_specs=blk,
        out_shape=jax.ShapeDtypeStruct((T, RWKV_W), BF16),
        scratch_shapes=[pltpu.VMEM((RWKV_PAIRS_PER_STEP, 128, 128), F32), pltpu.VMEM((BLK, lanes), F32)],
        compiler_params=_cp("arbitrary", "arbitrary", "arbitrary"),
        name="rwkv_rec",
    )(r, logw, k, v, kk, bb, bonus, g, row1(prm['rwkv_lnx_w']))


MLA_TQ = 1024
MLA_TK = 512
MLA_RQ = 256
MLA_VT = 80


def _mla_prep_kernel(p_ref, tq_ref, tk_ref, qn_ref, wq_ref, kn_ref, wk_ref, wv_ref, q_o, k_o, v_o):
    x = p_ref[...].astype(F32)
    qa = x[:, 0:256]
    ckv = x[:, 256:384]
    kpe = x[:, 384:512]
    qn = (qa * lax.rsqrt(jnp.mean(qa * qa, axis=-1, keepdims=True) + NORM_EPS) * qn_ref[...]).astype(BF16)
    cn = (ckv * lax.rsqrt(jnp.mean(ckv * ckv, axis=-1, keepdims=True) + NORM_EPS) * kn_ref[...]).astype(BF16)
    q = _dot(qn, wq_ref[...])
    kn = _dot(cn, wk_ref[...])
    vt = _dot_nt(wv_ref[...], cn)
    vrow = lax.rem(lax.broadcasted_iota(jnp.int32, vt.shape, 0), MLA_VT)
    v_o[...] = jnp.where(vrow >= MLA_V, 1.0, vt).astype(v_o.dtype)
    prod = kpe * tk_ref[...]
    lane = lax.broadcasted_iota(jnp.int32, prod.shape, 1)
    kr = jnp.where(lane >= 64, prod + pltpu.roll(prod, 64, 1), 0.0)
    tq = tq_ref[...]
    for h in range(MLA_HEADS):
        sl = slice(h * 128, (h + 1) * 128)
        q_o[:, sl] = (q[:, sl] * tq).astype(q_o.dtype)
        k_o[:, sl] = (kn[:, sl] + kr).astype(k_o.dtype)


def _mla_flash_kernel(q_ref, k_ref, v_ref, o_ref):
    i = pl.program_id(2)
    tq, tk, rq = MLA_TQ, MLA_TK, MLA_RQ
    keyc = lax.broadcasted_iota(jnp.int32, (tk, tk), 0) >> 6
    qryc = lax.broadcasted_iota(jnp.int32, (tk, tk), 1) >> 6
    diag_mask = keyc <= qryc

    chains = [(h, r0) for h in range(2) for r0 in range(0, tq, rq)]

    def tiles(off, carry, first_diag_row):
        def mask_of(r0):
            if first_diag_row is None or r0 >= first_diag_row + tk:
                return "all"
            if r0 < first_diag_row:
                return "none"
            return diag_mask[:, r0 - first_diag_row:r0 - first_diag_row + rq]

        active = [c for c, (_, r0) in enumerate(chains) if not isinstance(mask_of(r0), str) or mask_of(r0) == "all"]
        scores = {}
        for c in active:
            h, r0 = chains[c]
            hs = slice(h * 128, (h + 1) * 128)
            s = _dot_nt(k_ref[pl.ds(off, tk), hs], q_ref[r0:r0 + rq, hs])
            mk = mask_of(r0)
            if not isinstance(mk, str):
                s = jnp.where(mk, s, -1e30)
            scores[c] = s.astype(BF16)
        probs = {}
        for c in active:
            m = carry[c][0]
            m_new = jnp.maximum(m, jnp.max(scores[c], axis=0, keepdims=True).astype(F32))
            probs[c] = (m_new, jnp.exp(m - m_new), jnp.exp(scores[c] - m_new.astype(BF16)))
        out = list(carry)
        for c in active:
            h, r0 = chains[c]
            m_new, alpha, pexp = probs[c]
            vt = v_ref[h * MLA_VT:(h + 1) * MLA_VT, pl.ds(off, tk)]
            out[c] = (m_new, alpha * carry[c][1] + _dot(vt, pexp))
        return tuple(out)

    def body(j, carry):
        for d in range(tq // tk):
            carry = tiles(pl.multiple_of(j * tq + d * tk, tk), carry, None)
        return carry

    init = tuple((jnp.full((1, rq), -1e30, F32), jnp.zeros((MLA_VT, rq), F32)) for _ in chains)
    carry = lax.fori_loop(0, i, body, init)
    for d in range(tq // tk):
        carry = tiles(pl.multiple_of(i * tq + d * tk, tk), carry, d * tk)
    per_head = len(chains) // 2
    heads = []
    for h in range(2):
        acc = jnp.concatenate([c[1] for c in carry[h * per_head:(h + 1) * per_head]], axis=1)
        heads.append(acc[:MLA_V] / acc[MLA_V:MLA_V + 1])
    o_ref[...] = jnp.concatenate(heads, axis=0).T.astype(o_ref.dtype)


def mla(p, tab, q_norm, w_qb, kv_norm, w_kvb, B, S):
    T = B * S
    tm = 512
    scale = (MLA_NOPE + MLA_ROPE) ** -0.5
    wq3 = (w_qb * scale).reshape(MLA_Q_LORA, MLA_HEADS, MLA_NOPE + MLA_ROPE)
    x1 = wq3[:, :, MLA_NOPE:MLA_NOPE + 16]
    x2 = wq3[:, :, MLA_NOPE + 16:]
    wq = jnp.concatenate([wq3[:, :, :MLA_NOPE], x1, x2, x1, x2], axis=2).reshape(MLA_Q_LORA, 1024).astype(BF16)
    wkv3 = w_kvb.reshape(MLA_KV_LORA, MLA_HEADS, MLA_NOPE + MLA_V)
    wk = jnp.concatenate([wkv3[:, :, :MLA_NOPE], jnp.zeros((MLA_KV_LORA, MLA_HEADS, 64), w_kvb.dtype)],
                         axis=2).reshape(MLA_KV_LORA, 1024).astype(BF16)
    wv = jnp.concatenate([wkv3[:, :, MLA_NOPE:], jnp.zeros((MLA_KV_LORA, MLA_HEADS, MLA_VT - MLA_V), w_kvb.dtype)],
                         axis=2).reshape(MLA_KV_LORA, MLA_HEADS * MLA_VT).T.astype(BF16)
    full = lambda shape: pl.BlockSpec(shape, lambda i: (0, 0))
    q, k, v = pl.pallas_call(
        _mla_prep_kernel,
        grid=(T // tm,),
        in_specs=[pl.BlockSpec((tm, 512), lambda i: (i, P_MLA // 512)),
                  pl.BlockSpec((tm, 128), lambda i: (i, 2)),
                  pl.BlockSpec((tm, 128), lambda i: (i, 3)),
                  full((1, 256)), full((256, 1024)), full((1, 128)), full((128, 1024)),
                  full((MLA_HEADS * MLA_VT, 128))],
        out_specs=[pl.BlockSpec((tm, 1024), lambda i: (i, 0)),
                   pl.BlockSpec((tm, 1024), lambda i: (i, 0)),
                   pl.BlockSpec((MLA_HEADS * MLA_VT, tm), lambda i: (0, i))],
        out_shape=[jax.ShapeDtypeStruct((T, 1024), BF16), jax.ShapeDtypeStruct((T, 1024), BF16),
                   jax.ShapeDtypeStruct((MLA_HEADS * MLA_VT, T), BF16)],
        compiler_params=_cp("parallel"),
        name="mla_prep",
    )(p, tab, tab, q_norm[None, :].astype(F32), wq, kv_norm[None, :].astype(F32), wk, wv)

    nq = S // MLA_TQ
    return pl.pallas_call(
        _mla_flash_kernel,
        grid=(B, MLA_HEADS // 2, nq),
        in_specs=[pl.BlockSpec((MLA_TQ, 256), lambda b, j, i: (b * nq + i, j)),
                  pl.BlockSpec((S, 256), lambda b, j, i: (b, j)),
                  pl.BlockSpec((2 * MLA_VT, S), lambda b, j, i: (j, b))],
        out_specs=pl.BlockSpec((MLA_TQ, 128), lambda b, j, i: (b * nq + i, j)),
        out_shape=jax.ShapeDtypeStruct((T, 512), BF16),
        compiler_params=_cp("parallel", "parallel", "arbitrary"),
        name="mla_flash",
    )(q, k, v)


def _merge_kernel(yr_ref, yw_ref, ym_ref, g0_ref, g1_ref, g2_ref, h_ref, wb_ref, wo_ref, o_ref):
    merged = None
    for n, (y_ref, g_ref) in enumerate(((yr_ref, g0_ref), (yw_ref, g1_ref), (ym_ref, g2_ref))):
        term = _sigmoid(g_ref[...].astype(F32)) * _dot(y_ref[...], wb_ref[n])
        merged = term if merged is None else merged + term
    o_ref[...] = h_ref[...] + _dot(merged.astype(BF16), wo_ref[...])


def merge(y_ret, y_rwkv, y_mla, p, h, w_branch, w_o):
    T = h.shape[0]
    tm = 512
    yb = pl.BlockSpec((tm, 512), lambda i: (i, 0))
    gate = lambda n: pl.BlockSpec((tm, D_MODEL), lambda i, n=n: (i, P_GATE // D_MODEL + n))
    return pl.pallas_call(
        _merge_kernel,
        grid=(T // tm,),
        in_specs=[yb, yb, yb, gate(0), gate(1), gate(2),
                  pl.BlockSpec((tm, D_MODEL), lambda i: (i, 0)),
                  pl.BlockSpec((3, 512, D_MODEL), lambda i: (0, 0, 0)),
                  pl.BlockSpec((D_MODEL, D_MODEL), lambda i: (0, 0))],
        out_specs=pl.BlockSpec((tm, D_MODEL), lambda i: (i, 0)),
        out_shape=jax.ShapeDtypeStruct((T, D_MODEL), F32),
        compiler_params=_cp("parallel"),
        name="merge",
    )(y_ret, y_rwkv, y_mla, p, p, p, h, w_branch.astype(BF16), w_o.astype(BF16))


def _rms(x, g):
    return x * lax.rsqrt(jnp.mean(x * x, axis=-1, keepdims=True) + NORM_EPS) * g


FFN_TF = 256


def _ffn_kernel(h_ref, g_ref, wg_ref, wu_ref, wd_ref, o_ref):
    h = h_ref[...]
    hn = _rms(h, g_ref[...]).astype(BF16)
    acc = h
    for c in range(D_FF // FFN_TF):
        sl = slice(c * FFN_TF, (c + 1) * FFN_TF)
        a = _dot(hn, wg_ref[:, sl])
        u = _dot(hn, wu_ref[:, sl])
        acc = acc + _dot((a * _sigmoid(a) * u).astype(BF16), wd_ref[sl, :])
    o_ref[...] = acc


def dense_ffn(h, g, wg, wu, wd):
    T = h.shape[0]
    tm = 512
    full = lambda shape: pl.BlockSpec(shape, lambda i: (0, 0))
    return pl.pallas_call(
        _ffn_kernel,
        grid=(T // tm,),
        in_specs=[pl.BlockSpec((tm, D_MODEL), lambda i: (i, 0)), full((1, D_MODEL)),
                  full((D_MODEL, D_FF)), full((D_MODEL, D_FF)), full((D_FF, D_MODEL))],
        out_specs=pl.BlockSpec((tm, D_MODEL), lambda i: (i, 0)),
        out_shape=jax.ShapeDtypeStruct((T, D_MODEL), F32),
        compiler_params=_cp("parallel"),
        name="dense_ffn",
    )(h, g[None, :], wg.astype(BF16), wu.astype(BF16), wd.astype(BF16))


def _router_kernel(h_ref, g_ref, rhi_ref, rlo_ref, o_ref):
    hn = _rms(h_ref[...], g_ref[...])
    hi, mid, _ = _split3(hn)
    logits = _dot(hi, rhi_ref[...]) + (_dot(hi, rlo_ref[...]) + _dot(mid, rhi_ref[...]))
    lane = lax.broadcasted_iota(jnp.int32, logits.shape, 1)
    neg = jnp.float32(-1e30)
    logits = jnp.where(lane < N_EXPERTS, logits, neg)
    m1 = jnp.max(logits, axis=-1, keepdims=True)
    i1 = jnp.min(jnp.where(logits == m1, lane, 128), axis=-1, keepdims=True)
    rest = jnp.where(lane == i1, neg, logits)
    m2 = jnp.max(rest, axis=-1, keepdims=True)
    i2 = jnp.min(jnp.where(rest == m2, lane, 128), axis=-1, keepdims=True)
    e2 = jnp.exp(m2 - m1)
    w1 = 1.0 / (1.0 + e2)
    w2 = e2 / (1.0 + e2)
    o_ref[...] = (jnp.where(lane == i1, w1, 0.0) + jnp.where(lane == i2, w2, 0.0)
                  + jnp.where(lane == 8, w1, 0.0) + jnp.where(lane == 9, w2, 0.0)
                  + jnp.where(lane == 10, i1.astype(F32), 0.0) + jnp.where(lane == 11, i2.astype(F32), 0.0))


def router(h, g, w_router):
    T = h.shape[0]
    tm = 512
    wr = jnp.pad(w_router.astype(F32), ((0, 0), (0, 128 - N_EXPERTS)))
    rhi = wr.astype(BF16)
    rlo = (wr - rhi.astype(F32)).astype(BF16)
    full = lambda shape: pl.BlockSpec(shape, lambda i: (0, 0))
    return pl.pallas_call(
        _router_kernel,
        grid=(T // tm,),
        in_specs=[pl.BlockSpec((tm, D_MODEL), lambda i: (i, 0)), full((1, D_MODEL)),
                  full((D_MODEL, 128)), full((D_MODEL, 128))],
        out_specs=pl.BlockSpec((tm, 128), lambda i: (i, 0)),
        out_shape=jax.ShapeDtypeStruct((T, 128), F32),
        compiler_params=_cp("parallel"),
        name="router",
    )(h, g[None, :], rhi, rlo)


MOE_TM = 512
MOE_TF = 1792
MOE_DISPATCH_TM = 512


def _row_copy(src_ref, src_row, dst_ref, dst_row, sem):
    return pltpu.make_async_copy(src_ref.at[pl.ds(src_row, 1)], dst_ref.at[pl.ds(dst_row, 1)], sem)


def _dispatch_kernel(pos1_ref, pos2_ref, ends_ref, h_ref, g_ref, xs_ref, hn_ref, zero_ref, sems, zsem):
    i = pl.program_id(0)
    n = pl.num_programs(0)
    tm = h_ref.shape[0]

    @pl.when(i == 0)
    def _():
        zero_ref[...] = jnp.zeros_like(zero_ref)
        for wait in (False, True):
            for e in range(N_EXPERTS):
                lo = ends_ref[e - 1] if e else 0

                @pl.when(ends_ref[e] > lo)
                def _():
                    start = pl.multiple_of(ends_ref[e] - MOE_TM, MOE_TM)
                    cp = pltpu.make_async_copy(zero_ref, xs_ref.at[pl.ds(start, MOE_TM)], zsem)
                    cp.wait() if wait else cp.start()

            def unused_tile(t, c, wait=wait):
                cp = pltpu.make_async_copy(zero_ref, xs_ref.at[pl.ds(pl.multiple_of(t * MOE_TM, MOE_TM), MOE_TM)], zsem)
                cp.wait() if wait else cp.start()
                return c

            lax.fori_loop(ends_ref[N_EXPERTS - 1] // MOE_TM, xs_ref.shape[0] // MOE_TM, unused_tile, 0)

    def rows(tile, slot, wait):
        base = tile * tm

        def one(r, c):
            for pos_ref in (pos1_ref, pos2_ref):
                cp = _row_copy(hn_ref.at[slot], r, xs_ref, pos_ref[base + r], sems.at[slot])
                cp.wait() if wait else cp.start()
            return c

        lax.fori_loop(0, tm, one, 0, unroll=8)

    slot = lax.rem(i, 2)

    @pl.when(i >= 2)
    def _():
        rows(i - 2, slot, True)

    hn_ref[slot] = _rms(h_ref[...], g_ref[...])
    rows(i, slot, False)

    @pl.when(i == n - 1)
    def _():
        @pl.when(i >= 1)
        def _():
            rows(i - 1, 1 - slot, True)

        rows(i, slot, True)


def _moe_kernel(te_ref, nv_ref, xs_ref, wg_ref, wu_ref, wd_ref, y_ref, xb_ref, acc_ref):
    del te_ref
    i = pl.program_id(0)
    f = pl.program_id(1)

    @pl.when(i < nv_ref[0])
    def _():
        @pl.when(f == 0)
        def _():
            xb_ref[...] = xs_ref[...].astype(BF16)

        xb = xb_ref[...]
        a = _dot(xb, wg_ref[0])
        u = _dot(xb, wu_ref[0])
        t = _dot((a * _sigmoid(a) * u).astype(BF16), wd_ref[0])

        @pl.when(f == 0)
        def _():
            acc_ref[...] = t

        @pl.when(f > 0)
        def _():
            acc_ref[...] += t

        @pl.when(f == pl.num_programs(1) - 1)
        def _():
            y_ref[...] = acc_ref[...]

    @pl.when(i >= nv_ref[0])
    def _():
        y_ref[...] = jnp.zeros_like(y_ref)


def _combine_kernel(pos1_ref, pos2_ref, h_ref, rt_ref, fin_ref, y_ref, o_ref, ybuf_ref, sems):
    i = pl.program_id(0)
    n = pl.num_programs(0)
    tm = h_ref.shape[0]

    def rows(tile, slot, wait):
        base = tile * tm

        def one(r, c):
            for k, pos_ref in enumerate((pos1_ref, pos2_ref)):
                cp = _row_copy(y_ref, pos_ref[base + r], ybuf_ref.at[slot, k], r, sems.at[slot])
                cp.wait() if wait else cp.start()
            return c

        lax.fori_loop(0, tm, one, 0, unroll=8)

    slot = lax.rem(i, 2)

    @pl.when(i == 0)
    def _():
        rows(i, slot, False)

    @pl.when(i + 1 < n)
    def _():
        rows(i + 1, 1 - slot, False)

    rows(i, slot, True)
    rt = rt_ref[...]
    lane = lax.broadcasted_iota(jnp.int32, rt.shape, 1)
    w1 = jnp.sum(jnp.where(lane == 8, rt, 0.0), axis=-1, keepdims=True)
    w2 = jnp.sum(jnp.where(lane == 9, rt, 0.0), axis=-1, keepdims=True)
    o_ref[...] = _rms(h_ref[...] + (w1 * ybuf_ref[slot, 0] + w2 * ybuf_ref[slot, 1]), fin_ref[...])


def _moe_plan(rt, n_tiles):
    e1 = rt[:, 10].astype(jnp.int32)
    e2 = rt[:, 11].astype(jnp.int32)
    eid = jnp.arange(N_EXPERTS, dtype=jnp.int32)[None, :]
    m1 = (e1[:, None] == eid).astype(jnp.int32)
    m2 = (e2[:, None] == eid).astype(jnp.int32)
    m = m1 + m2
    counts = jnp.sum(m, axis=0)
    rank = jnp.cumsum(m, axis=0) - m
    padded = ((counts + MOE_TM - 1) // MOE_TM) * MOE_TM
    ends = jnp.cumsum(padded)
    pos_te = (ends - padded)[None, :] + rank
    pos1 = jnp.sum(m1 * pos_te, axis=1).astype(jnp.int32)
    pos2 = jnp.sum(m2 * pos_te, axis=1).astype(jnp.int32)
    nvalid = (ends[-1] // MOE_TM).astype(jnp.int32)
    starts = jnp.minimum(jnp.arange(n_tiles, dtype=jnp.int32), nvalid - 1) * MOE_TM
    tile_expert = jnp.sum((starts[:, None] >= ends[None, :]).astype(jnp.int32), axis=1)
    return pos1, pos2, ends.astype(jnp.int32), tile_expert.astype(jnp.int32), nvalid.reshape(1)


def moe_final(h, g, rt, wg, wu, wd, final_g):
    T = h.shape[0]
    n_tiles = (2 * T) // MOE_TM + N_EXPERTS
    n_rows = n_tiles * MOE_TM
    nf = D_FF_EXPERT // MOE_TF
    pos1, pos2, ends, tile_expert, nvalid = _moe_plan(rt, n_tiles)

    tm = MOE_DISPATCH_TM
    xs = pl.pallas_call(
        _dispatch_kernel,
        grid_spec=pltpu.PrefetchScalarGridSpec(
            num_scalar_prefetch=3,
            grid=(T // tm,),
            in_specs=[pl.BlockSpec((tm, D_MODEL), lambda i, p1, p2, en: (i, 0)),
                      pl.BlockSpec((1, D_MODEL), lambda i, p1, p2, en: (0, 0))],
            out_specs=pl.BlockSpec(memory_space=pl.ANY),
            scratch_shapes=[pltpu.VMEM((2, tm, D_MODEL), F32), pltpu.VMEM((MOE_TM, D_MODEL), F32),
                            pltpu.SemaphoreType.DMA((2,)), pltpu.SemaphoreType.DMA(())]),
        out_shape=jax.ShapeDtypeStruct((n_rows, D_MODEL), F32),
        compiler_params=_cp("arbitrary"),
        name="moe_dispatch",
    )(pos1, pos2, ends, h, g[None, :])

    def row_tile(i, f, te, nv):
        return (jnp.minimum(i, nv[0] - 1), 0)

    def w_up(i, f, te, nv):
        return (te[i], 0, jnp.where(i < nv[0], f, nf - 1))

    def w_down(i, f, te, nv):
        return (te[i], jnp.where(i < nv[0], f, nf - 1), 0)

    y = pl.pallas_call(
        _moe_kernel,
        grid_spec=pltpu.PrefetchScalarGridSpec(
            num_scalar_prefetch=2,
            grid=(n_tiles, nf),
            in_specs=[pl.BlockSpec((MOE_TM, D_MODEL), row_tile),
                      pl.BlockSpec((1, D_MODEL, MOE_TF), w_up),
                      pl.BlockSpec((1, D_MODEL, MOE_TF), w_up),
                      pl.BlockSpec((1, MOE_TF, D_MODEL), w_down)],
            out_specs=pl.BlockSpec((MOE_TM, D_MODEL), lambda i, f, te, nv: (i, 0)),
            scratch_shapes=[pltpu.VMEM((MOE_TM, D_MODEL), BF16), pltpu.VMEM((MOE_TM, D_MODEL), F32)]),
        out_shape=jax.ShapeDtypeStruct((n_rows, D_MODEL), F32),
        compiler_params=_cp("arbitrary", "arbitrary"),
        name="moe_experts",
    )(tile_expert, nvalid, xs, wg.astype(BF16), wu.astype(BF16), wd.astype(BF16))

    return pl.pallas_call(
        _combine_kernel,
        grid_spec=pltpu.PrefetchScalarGridSpec(
            num_scalar_prefetch=2,
            grid=(T // tm,),
            in_specs=[pl.BlockSpec((tm, D_MODEL), lambda i, p1, p2: (i, 0)),
                      pl.BlockSpec((tm, 128), lambda i, p1, p2: (i, 0)),
                      pl.BlockSpec((1, D_MODEL), lambda i, p1, p2: (0, 0)),
                      pl.BlockSpec(memory_space=pl.ANY)],
            out_specs=pl.BlockSpec((tm, D_MODEL), lambda i, p1, p2: (i, 0)),
            scratch_shapes=[pltpu.VMEM((2, 2, tm, D_MODEL), F32), pltpu.SemaphoreType.DMA((2,))]),
        out_shape=jax.ShapeDtypeStruct((T, D_MODEL), F32),
        compiler_params=_cp("arbitrary"),
        name="moe_combine",
    )(pos1, pos2, h, rt, final_g[None, :], y)


def _final_norm_kernel(h_ref, g_ref, o_ref):
    o_ref[...] = _rms(h_ref[...], g_ref[...])


def final_norm(h, g):
    T = h.shape[0]
    tm = 1024
    return pl.pallas_call(
        _final_norm_kernel,
        grid=(T // tm,),
        in_specs=[pl.BlockSpec((tm, D_MODEL), lambda i: (i, 0)), pl.BlockSpec((1, D_MODEL), lambda i: (0, 0))],
        out_specs=pl.BlockSpec((tm, D_MODEL), lambda i: (i, 0)),
        out_shape=jax.ShapeDtypeStruct((T, D_MODEL), F32),
        compiler_params=_cp("parallel"),
        name="final_norm",
    )(h, g[None, :])


def kernel(x, positions, attn_norm, w_in, rwkv_mu, rwkv_w0, rwkv_w2, rwkv_a0, rwkv_a2, rwkv_g2, rwkv_k_k, rwkv_k_a,
           rwkv_r_k, rwkv_lnx_w, mla_q_norm, mla_w_qb, mla_kv_norm, mla_w_kvb, w_branch, w_o, ffn_norm, ffn_w_gate,
           ffn_w_up, ffn_w_down, moe_router, moe_w_gate, moe_w_up, moe_w_down, final_norm_g):
    B, S, D = x.shape
    T = B * S
    depth = attn_norm.shape[0]
    rwkv_all = dict(rwkv_mu=rwkv_mu, rwkv_w0=rwkv_w0, rwkv_w2=rwkv_w2, rwkv_a0=rwkv_a0, rwkv_a2=rwkv_a2,
                    rwkv_g2=rwkv_g2, rwkv_k_k=rwkv_k_k, rwkv_k_a=rwkv_k_a, rwkv_r_k=rwkv_r_k,
                    rwkv_lnx_w=rwkv_lnx_w)
    tab = rope_tables(positions)
    h = x.reshape(T, D)
    out = None
    for l in range(depth):
        p = norm_in_proj(h, attn_norm[l][None, :], relayout_w_in(w_in[l]))
        y_ret = retention(p, tab, B, S)
        y_rwkv = rwkv7(p, {k: v[l] for k, v in rwkv_all.items()}, B, S)
        y_mla = mla(p, tab, mla_q_norm[l], mla_w_qb[l], mla_kv_norm[l], mla_w_kvb[l], B, S)
        h = merge(y_ret, y_rwkv, y_mla, p, h, w_branch[l], w_o[l])
        last = l == depth - 1
        if l % 2 == 0:
            h = dense_ffn(h, ffn_norm[l], ffn_w_gate[l // 2], ffn_w_up[l // 2], ffn_w_down[l // 2])
            if last:
                out = final_norm(h, final_norm_g)
        else:
            rt = router(h, ffn_norm[l], moe_router[l // 2])
            out_l = moe_final(h, ffn_norm[l], rt, moe_w_gate[l // 2], moe_w_up[l // 2], moe_w_down[l // 2],
                              final_norm_g)
            if last:
                out = out_l
            else:
                raise NotImplementedError("a MoE layer that is not the last layer")
    return out.reshape(B, S, D)
```
